```python
import math, functools
import jax, jax.numpy as jnp
from jax import lax
import numpy as np

D_MODEL = 1024
BATCH = 4
SEQ = 4096
DEPTH = 1
DEC_BATCH = 128
DEC_SEQ = 4
PAST_LEN = 2048
PAGE_SIZE = 128

HEAD_DIM = 64
RWKV_HEADS = D_MODEL // HEAD_DIM
RWKV_DIM = RWKV_HEADS * HEAD_DIM
DECAY_LORA = 64
AAA_LORA = 64
GATE_LORA = 128
MOBA_HEADS = D_MODEL // HEAD_DIM
MOBA_DIM = MOBA_HEADS * HEAD_DIM
MOBA_BLOCK = 256
MOBA_TOPK = 3
MOBA_QCHUNK = 64
ROPE_THETA = 10000.0
D_FF = 11 * D_MODEL // 4
N_ADA = 9
MACARON_W = 0.5
RMS_EPS = 1e-6
LNX_EPS = 64e-5
SHIFT_W = 3 * RWKV_DIM + DECAY_LORA + AAA_LORA + GATE_LORA
IN_W = SHIFT_W + 3 * MOBA_DIM + 2 * D_MODEL

kernel_name = "rwkv7_moba_gated_macaron_step"


def rms_norm(x, g):
    x32 = x.astype(jnp.float32)
    y = x32 * lax.rsqrt(jnp.mean(x32 * x32, axis=-1, keepdims=True) + RMS_EPS)
    return (y * g.astype(jnp.float32)).astype(x.dtype)


def rope(x, pos):
    half = HEAD_DIM // 2
    inv = ROPE_THETA ** (-jnp.arange(half, dtype=jnp.float32) / half)
    ang = pos.astype(jnp.float32)[:, None] * inv[None, :]
    cos = jnp.cos(ang)[None, :, None, :]
    sin = jnp.sin(ang)[None, :, None, :]
    x32 = x.astype(jnp.float32)
    x1, x2 = x32[..., :half], x32[..., half:]
    return jnp.concatenate([x1 * cos - x2 * sin, x1 * sin + x2 * cos], axis=-1).astype(x.dtype)


def swiglu(h, w_gu, w_down):
    gate, up = jnp.split(h @ w_gu, 2, axis=-1)
    return (jax.nn.silu(gate) * up) @ w_down


def wkv7_scan(S0, r, w, k, v, kk, a):
    def step(S, inp):
        r_t, w_t, k_t, v_t, kk_t, a_t = inp
        sa = jnp.einsum('bhvk,bhk->bhv', S, kk_t)
        S = (S * w_t[:, :, None, :]
             - sa[..., None] * (kk_t * a_t)[:, :, None, :]
             + v_t[..., None] * k_t[:, :, None, :])
        return S, jnp.einsum('bhvk,bhk->bhv', S, r_t)
    xs = tuple(jnp.moveaxis(t, 1, 0) for t in (r, w, k, v, kk, a))
    S, o = lax.scan(step, S0, xs)
    return S, jnp.moveaxis(o, 0, 1)


def rwkv7_branch(p, shift0, S0, lp):
    B, T, _ = p.shape
    f32 = jnp.float32
    prev = jnp.concatenate([shift0[:, None, :].astype(p.dtype), p[:, :-1]], axis=1)
    xm = p + (prev - p) * lp["mu_shift"]
    c1 = 3 * RWKV_DIM
    r, k, v, xw, xa, xg = jnp.split(
        xm, [RWKV_DIM, 2 * RWKV_DIM, c1, c1 + DECAY_LORA, c1 + DECAY_LORA + AAA_LORA], axis=-1)
    w_log = -jax.nn.softplus(-(lp["w0"] + jnp.tanh(xw) @ lp["w2_decay"]).astype(f32)) - 0.5
    decay = jnp.exp(-jnp.exp(w_log))
    a = jax.nn.sigmoid((lp["a0"] + xa @ lp["a2"]).astype(f32))
    g = (jax.nn.sigmoid(xg) @ lp["g2"]).astype(f32)
    r, k, v = r.astype(f32), k.astype(f32), v.astype(f32)
    kk = k * lp["k_k"].astype(f32)
    k = k * (1.0 + (a - 1.0) * lp["k_a"].astype(f32))
    heads = lambda t: t.reshape(B, T, RWKV_HEADS, HEAD_DIM)
    kk = heads(kk)
    kk = kk / jnp.maximum(jnp.sqrt(jnp.sum(kk * kk, axis=-1, keepdims=True)), 1e-12)
    rh, kh, vh, ah, wh = heads(r), heads(k), heads(v), heads(a), heads(decay)
    S, o = wkv7_scan(S0.astype(f32), rh, wh, kh, vh, kk, ah)
    mu = jnp.mean(o, axis=-1, keepdims=True)
    var = jnp.mean(jnp.square(o - mu), axis=-1, keepdims=True)
    o = ((o - mu) * lax.rsqrt(var + LNX_EPS)).reshape(B, T, RWKV_DIM)
    o = o * lp["lnx_w"].astype(f32) + lp["lnx_b"].astype(f32)
    bonus = jnp.sum(rh * kh * lp["r_k"].astype(f32), axis=-1, keepdims=True) * vh
    y = ((o + bonus.reshape(B, T, RWKV_DIM)) * g).astype(p.dtype) @ lp["w_o_rwkv"]
    return y, S, p[:, -1]


def moba_core(q, qpos, kb, vb, kmean):
    H, Q = q.shape[0], q.shape[1]
    nb = kb.shape[1]
    topk = min(MOBA_TOPK, nb)
    qblk = qpos // MOBA_BLOCK
    gate = jnp.einsum('hqd,hnd->hqn', q.astype(jnp.float32), kmean)
    is_past = jnp.arange(nb)[None, :] < qblk[:, None]
    gate = jnp.where(is_past[None], gate, -jnp.inf)
    _, top = lax.top_k(gate, topk)
    own = jnp.broadcast_to(qblk[None, :, None], (H, Q, 1)).astype(top.dtype)
    sel = jnp.concatenate([top, own], axis=-1)
    slot_ok = jnp.concatenate(
        [jnp.arange(topk)[None, :] < jnp.minimum(qblk, topk)[:, None], jnp.ones((Q, 1), bool)], axis=-1)
    kg = jax.vmap(lambda t, i: t[i])(kb, sel)
    vg = jax.vmap(lambda t, i: t[i])(vb, sel)
    kpos = sel[..., None] * MOBA_BLOCK + jnp.arange(MOBA_BLOCK)
    ok = slot_ok[None, :, :, None] & (kpos <= qpos[None, :, None, None])
    logits = jnp.einsum('hqd,hqsbd->hqsb', q, kg, preferred_element_type=jnp.float32) * (HEAD_DIM ** -0.5)
    logits = jnp.where(ok, logits, -jnp.inf)
    p = jax.nn.softmax(logits.reshape(H, Q, -1), axis=-1).reshape(logits.shape)
    out = jnp.einsum('hqsb,hqsbd->hqd', p.astype(vg.dtype), vg, preferred_element_type=jnp.float32)
    return out.astype(q.dtype)


def moba_prompt(q, k, v, pos):
    B, S = q.shape[0], q.shape[1]
    nb = -(-S // MOBA_BLOCK)
    pad = nb * MOBA_BLOCK - S

    def blocks(t):
        t = jnp.pad(t, ((0, 0), (0, pad), (0, 0), (0, 0)))
        return t.reshape(B, nb, MOBA_BLOCK, MOBA_HEADS, HEAD_DIM).transpose(0, 3, 1, 2, 4)

    kb, vb = blocks(k), blocks(v)
    qc = min(MOBA_QCHUNK, S)
    nq = S // qc
    qs = q.reshape(B, nq, qc, MOBA_HEADS, HEAD_DIM).transpose(0, 1, 3, 2, 4)
    ps = pos.reshape(nq, qc)

    def per_seq(args):
        qb, kbb, vbb = args
        kmean = jnp.mean(kbb.astype(jnp.float32), axis=2)
        return lax.map(lambda a: moba_core(a[0], a[1], kbb, vbb, kmean), (qb, ps))

    o = lax.map(per_seq, (qs, kb, vb))
    return o.transpose(0, 1, 3, 2, 4).reshape(B, S, MOBA_DIM)


def moba_sample(q, k, v, pos, cache_k, cache_v, page_table, layer):
    Bd, T = q.shape[0], q.shape[1]
    past = page_table.shape[1] * PAGE_SIZE
    L = past + T
    nb = -(-L // MOBA_BLOCK)
    pad = nb * MOBA_BLOCK - L

    def per_seq(args):
        qb, kn, vn, pt = args

        def full(pool, new):
            rows = pool[layer, pt].reshape(past, MOBA_HEADS, HEAD_DIM)
            rows = jnp.concatenate([rows, new.astype(rows.dtype)], axis=0)
            rows = jnp.pad(rows, ((0, pad), (0, 0), (0, 0)))
            return rows.reshape(nb, MOBA_BLOCK, MOBA_HEADS, HEAD_DIM).transpose(2, 0, 1, 3)

        kbb, vbb = full(cache_k, kn), full(cache_v, vn)
        kmean = jnp.mean(kbb.astype(jnp.float32), axis=2)
        return moba_core(qb.transpose(1, 0, 2), pos, kbb, vbb.astype(qb.dtype), kmean)

    o = lax.map(per_seq, (q, k, v, page_table))
    return o.transpose(0, 2, 1, 3).reshape(Bd, T, MOBA_DIM)


def mixer(h, lp, S0, shift0, attend, pos):
    B, T, _ = h.shape
    proj = h @ lp["w_in"]
    p_r, q, k, v, g_a, g_b = jnp.split(
        proj, [SHIFT_W, SHIFT_W + MOBA_DIM, SHIFT_W + 2 * MOBA_DIM, SHIFT_W + 3 * MOBA_DIM,
               SHIFT_W + 3 * MOBA_DIM + D_MODEL], axis=-1)
    y_a, S, shift = rwkv7_branch(p_r, shift0, S0, lp)
    heads = lambda t: t.reshape(B, T, MOBA_HEADS, HEAD_DIM)
    q, k, v = rope(heads(q), pos), rope(heads(k), pos), heads(v)
    y_b = attend(q, k, v) @ lp["w_o_moba"]
    y = (jax.nn.sigmoid(g_a) * y_a + jax.nn.sigmoid(g_b) * y_b) @ lp["w_out"]
    return y, (k, v, S, shift)


def layer_forward(x, c, lp, S0, shift0, attend, pos):
    mod = (jax.nn.silu(c) @ lp["w_ada"] + lp["b_ada"]).reshape(c.shape[0], N_ADA, D_MODEL)
    sh1, sc1, g1, sh2, sc2, g2, sh3, sc3, g3 = [mod[:, i, None, :] for i in range(N_ADA)]
    h = rms_norm(x, lp["n1_pre"]) * (1.0 + sc1) + sh1
    x = x + MACARON_W * g1 * rms_norm(swiglu(h, lp["w1_gu"], lp["w1_down"]), lp["n1_post"])
    h = rms_norm(x, lp["n2_pre"]) * (1.0 + sc2) + sh2
    y, st = mixer(h, lp, S0, shift0, attend, pos)
    x = x + g2 * rms_norm(y, lp["n2_post"])
    h = rms_norm(x, lp["n3_pre"]) * (1.0 + sc3) + sh3
    x = x + MACARON_W * g3 * rms_norm(swiglu(h, lp["w3_gu"], lp["w3_down"]), lp["n3_post"])
    return x, st


def setup_inputs(seed: int = 0) -> dict:
    key = jax.random.key(seed)
    ks = iter(jax.random.split(key, 64))
    f32 = jnp.float32

    def normal(shape, scale):
        return scale * jax.random.normal(next(ks), shape, f32)

    Lr = DEPTH
    n_pages = PAST_LEN // PAGE_SIZE
    n_used = DEC_BATCH * n_pages
    n_pool = n_used + n_used // 4
    perm = jax.random.permutation(next(ks), n_pool)[:n_used]
    page_table = perm.reshape(DEC_BATCH, n_pages).astype(jnp.int32)
    gain = lambda: 1.0 + normal((Lr, D_MODEL), 0.05)
    return {
        "x_prompt": normal((BATCH, SEQ, D_MODEL), 1.0),
        "x_sample": normal((DEC_BATCH, DEC_SEQ, D_MODEL), 1.0),
        "cache_k": normal((Lr, n_pool, PAGE_SIZE, MOBA_HEADS, HEAD_DIM), 1.0),
        "cache_v": normal((Lr, n_pool, PAGE_SIZE, MOBA_HEADS, HEAD_DIM), 1.0),
        "state_rwkv": normal((Lr, DEC_BATCH, RWKV_HEADS, HEAD_DIM, HEAD_DIM), 0.3),
        "state_shift": normal((Lr, DEC_BATCH, SHIFT_W), 1.0),
        "page_table": page_table,
        "c_prompt": normal((BATCH, D_MODEL), 1.0),
        "c_sample": normal((DEC_BATCH, D_MODEL), 1.0),
        "w_ada": normal((Lr, D_MODEL, N_ADA * D_MODEL), D_MODEL ** -0.5),
        "b_ada": normal((Lr, N_ADA * D_MODEL), 0.02),
        "n1_pre": gain(),
        "n1_post": gain(),
        "w1_gu": normal((Lr, D_MODEL, 2 * D_FF), D_MODEL ** -0.5),
        "w1_down": normal((Lr, D_FF, D_MODEL), D_FF ** -0.5),
        "n2_pre": gain(),
        "n2_post": gain(),
        "w_in": normal((Lr, D_MODEL, IN_W), D_MODEL ** -0.5),
        "mu_shift": jax.random.uniform(next(ks), (Lr, SHIFT_W), f32),
        "w0": jax.random.uniform(next(ks), (Lr, RWKV_DIM), f32, -6.5, -1.5),
        "w2_decay": normal((Lr, DECAY_LORA, RWKV_DIM), 0.1),
        "a0": normal((Lr, RWKV_DIM), 0.1),
        "a2": normal((Lr, AAA_LORA, RWKV_DIM), AAA_LORA ** -0.5),
        "g2": normal((Lr, GATE_LORA, RWKV_DIM), GATE_LORA ** -0.5),
        "k_k": 0.85 + normal((Lr, RWKV_DIM), 0.02),
        "k_a": 1.0 + normal((Lr, RWKV_DIM), 0.02),
        "r_k": normal((Lr, RWKV_HEADS, HEAD_DIM), 0.1),
        "lnx_w": 1.0 + normal((Lr, RWKV_DIM), 0.05),
        "lnx_b": normal((Lr, RWKV_DIM), 0.02),
        "w_o_rwkv": normal((Lr, RWKV_DIM, D_MODEL), RWKV_DIM ** -0.5),
        "w_o_moba": normal((Lr, MOBA_DIM, D_MODEL), MOBA_DIM ** -0.5),
        "w_out": normal((Lr, D_MODEL, D_MODEL), D_MODEL ** -0.5),
        "n3_pre": gain(),
        "n3_post": gain(),
        "w3_gu": normal((Lr, D_MODEL, 2 * D_FF), D_MODEL ** -0.5),
        "w3_down": normal((Lr, D_FF, D_MODEL), D_FF ** -0.5),
    }


def reference(x_prompt, x_sample, cache_k, cache_v, state_rwkv, state_shift, page_table, c_prompt, c_sample,
              w_ada, b_ada, n1_pre, n1_post, w1_gu, w1_down, n2_pre, n2_post, w_in, mu_shift, w0, w2_decay,
              a0, a2, g2, k_k, k_a, r_k, lnx_w, lnx_b, w_o_rwkv, w_o_moba, w_out, n3_pre, n3_post, w3_gu,
              w3_down):
    past_len = page_table.shape[1] * PAGE_SIZE
    pos_p = jnp.arange(x_prompt.shape[1], dtype=jnp.int32)
    pos_s = past_len + jnp.arange(x_sample.shape[1], dtype=jnp.int32)
    bp = x_prompt.shape[0]
    yp, ys = x_prompt, x_sample
    kp_l, vp_l, sp_l, hp_l, ks_l, vs_l, ss_l, hs_l = [], [], [], [], [], [], [], []
    for l in range(DEPTH):
        lp = dict(w_ada=w_ada[l], b_ada=b_ada[l], n1_pre=n1_pre[l], n1_post=n1_post[l], w1_gu=w1_gu[l],
                  w1_down=w1_down[l], n2_pre=n2_pre[l], n2_post=n2_post[l], w_in=w_in[l], mu_shift=mu_shift[l],
                  w0=w0[l], w2_decay=w2_decay[l], a0=a0[l], a2=a2[l], g2=g2[l], k_k=k_k[l], k_a=k_a[l],
                  r_k=r_k[l], lnx_w=lnx_w[l], lnx_b=lnx_b[l], w_o_rwkv=w_o_rwkv[l], w_o_moba=w_o_moba[l],
                  w_out=w_out[l], n3_pre=n3_pre[l], n3_post=n3_post[l], w3_gu=w3_gu[l], w3_down=w3_down[l])
        S0p = jnp.zeros((bp, RWKV_HEADS, HEAD_DIM, HEAD_DIM), jnp.float32)
        h0p = jnp.zeros((bp, SHIFT_W), x_prompt.dtype)
        yp, (kp, vp, Sp, hp) = layer_forward(
            yp, c_prompt, lp, S0p, h0p, functools.partial(moba_prompt, pos=pos_p), pos_p)
        attend_s = functools.partial(moba_sample, pos=pos_s, cache_k=cache_k, cache_v=cache_v,
                                     page_table=page_table, layer=l)
        ys, (ks_, vs_, Ss, hs) = layer_forward(ys, c_sample, lp, state_rwkv[l], state_shift[l], attend_s, pos_s)
        kp_l.append(kp); vp_l.append(vp); sp_l.append(Sp); hp_l.append(hp)
        ks_l.append(ks_); vs_l.append(vs_); ss_l.append(Ss); hs_l.append(hs)
    k_prompt, v_prompt = jnp.stack(kp_l), jnp.stack(vp_l)
    rwkv_prompt, shift_prompt = jnp.stack(sp_l), jnp.stack(hp_l)
    k_sample, v_sample = jnp.stack(ks_l), jnp.stack(vs_l)
    rwkv_sample, shift_sample = jnp.stack(ss_l), jnp.stack(hs_l)
    return (yp, ys, k_prompt, v_prompt, rwkv_prompt, shift_prompt, k_sample, v_sample, rwkv_sample, shift_sample)
```

```python
import functools

import jax
import jax.numpy as jnp
from jax import lax
from jax.experimental import pallas as pl
from jax.experimental.pallas import tpu as pltpu

D_MODEL = 1024
HEAD_DIM = 64
N_HEADS = D_MODEL // HEAD_DIM
PAIR_W = 2 * HEAD_DIM
N_PAIRS = N_HEADS // 2
DECAY_LORA = 64
AAA_LORA = 64
GATE_LORA = 128
SHIFT_W = 3 * D_MODEL + DECAY_LORA + AAA_LORA + GATE_LORA
IN_W = SHIFT_W + 3 * D_MODEL + 2 * D_MODEL
Q_COL = SHIFT_W
K_COL = SHIFT_W + D_MODEL
V_COL = SHIFT_W + 2 * D_MODEL
GA_COL = SHIFT_W + 3 * D_MODEL
GB_COL = SHIFT_W + 4 * D_MODEL
D_FF = 11 * D_MODEL // 4
N_ADA = 9
MACARON_W = 0.5
RMS_EPS = 1e-6
LNX_EPS = 64e-5
MOBA_BLOCK = 256
MOBA_TOPK = 3
PAGE_SIZE = 128
ROPE_THETA = 10000.0

V7X_LANES = 128
VMEM_LIMIT = 56 * 1024 * 1024

BF16 = jnp.bfloat16
F32 = jnp.float32
NEG_INF = float("-inf")


def _cparams(*sem):
    return pltpu.CompilerParams(dimension_semantics=sem, vmem_limit_bytes=VMEM_LIMIT)


def _rms(x, g):
    return x * lax.rsqrt(jnp.mean(x * x, axis=-1, keepdims=True) + RMS_EPS) * g


def _dot(a, b):
    return jnp.dot(a, b, preferred_element_type=F32)


def _dot_nt(a, b):
    return lax.dot_general(a, b, (((1,), (1,)), ((), ())), preferred_element_type=F32)


def _dot_tn(a, b):
    return lax.dot_general(a, b, (((0,), (0,)), ((), ())), preferred_element_type=F32)


ADA_TN = 1536


def _ada_kernel(c_ref, w_ref, b_ref, o_ref):
    c = c_ref[...]
    a = (c * jax.nn.sigmoid(c)).astype(BF16)
    o_ref[...] = _dot(a, w_ref[...].astype(BF16)) + b_ref[...]


def _ada(c_all, w_ada, b_ada):
    rows = c_all.shape[0]
    n_out = w_ada.shape[1]
    return pl.pallas_call(
        _ada_kernel,
        out_shape=jax.ShapeDtypeStruct((rows, n_out), F32),
        grid=(n_out // ADA_TN,),
        in_specs=[
            pl.BlockSpec((rows, D_MODEL), lambda j: (0, 0)),
            pl.BlockSpec((D_MODEL, ADA_TN), lambda j: (0, j)),
            pl.BlockSpec((1, ADA_TN), lambda j: (0, j)),
        ],
        out_specs=pl.BlockSpec((rows, ADA_TN), lambda j: (0, j)),
        compiler_params=_cparams("arbitrary"),
        name="ada_mod",
    )(c_all, w_ada, b_ada.reshape(1, n_out))


def _mod_spec(mod, which, tm, tiles_per_group):
    rows_in_group = mod.shape[2]
    if rows_in_group == 1:
        return pl.BlockSpec((None, None, 1, D_MODEL), lambda i, *_: (which, i // tiles_per_group, 0, 0))
    return pl.BlockSpec((None, None, tm, D_MODEL), lambda i, *_: (which, 0, i, 0))


def _vec_spec():
    return pl.BlockSpec((1, D_MODEL), lambda i, *_: (0, 0))


FFN_TF = D_FF // 2


def _ffn_kernel(x_ref, sh_ref, sc_ref, gt_ref, npre_ref, npost_ref, wg_ref, wu_ref, wd_ref, o_ref, h_scr, acc_scr):
    j = pl.program_id(1)

    @pl.when(j == 0)
    def _():
        h = _rms(x_ref[...], npre_ref[...]) * (1.0 + sc_ref[...]) + sh_ref[...]
        h_scr[...] = h.astype(BF16)
        acc_scr[...] = jnp.zeros_like(acc_scr)

    h = h_scr[...]
    g = _dot(h, wg_ref[...])
    u = _dot(h, wu_ref[...])
    a = (g * jax.nn.sigmoid(g) * u).astype(BF16)
    acc_scr[...] += _dot(a, wd_ref[...])

    @pl.when(j == pl.num_programs(1) - 1)
    def _():
        y = _rms(acc_scr[...], npost_ref[...])
        o_ref[...] = x_ref[...] + MACARON_W * gt_ref[...] * y


def _ffn(x, mod, which0, n_pre, n_post, wgu, wd, *, tm, tiles_per_group):
    rows = x.shape[0]
    nf = D_FF // FFN_TF
    row_spec = pl.BlockSpec((tm, D_MODEL), lambda i, j: (i, 0))
    return pl.pallas_call(
        _ffn_kernel,
        out_shape=jax.ShapeDtypeStruct((rows, D_MODEL), F32),
        grid=(rows // tm, nf),
        in_specs=[
            row_spec,
            _mod_spec(mod, which0, tm, tiles_per_group),
            _mod_spec(mod, which0 + 1, tm, tiles_per_group),
            _mod_spec(mod, which0 + 2, tm, tiles_per_group),
            _vec_spec(),
            _vec_spec(),
            pl.BlockSpec((D_MODEL, FFN_TF), lambda i, j: (0, j)),
            pl.BlockSpec((D_MODEL, FFN_TF), lambda i, j: (0, j + nf)),
            pl.BlockSpec((FFN_TF, D_MODEL), lambda i, j: (j, 0)),
        ],
        out_specs=row_spec,
        scratch_shapes=[pltpu.VMEM((tm, D_MODEL), BF16), pltpu.VMEM((tm, D_MODEL), F32)],
        compiler_params=_cparams("parallel", "arbitrary"),
        name="ffn",
    )(x, mod, mod, mod, n_pre, n_post, wgu, wgu, wd)


INPROJ_WIDTHS = (SHIFT_W, 3 * D_MODEL, 2 * D_MODEL)
INPROJ_SPLIT = 2


def _inproj_kernel(x_ref, sh_ref, sc_ref, npre_ref, w0_ref, w1_ref, w2_ref, o0_ref, o1_ref, o2_ref, h_scr):
    j = pl.program_id(1)

    @pl.when(j == 0)
    def _():
        h = _rms(x_ref[...], npre_ref[...]) * (1.0 + sc_ref[...]) + sh_ref[...]
        h_scr[...] = h.astype(BF16)

    for sec, (w_ref, o_ref) in enumerate(((w0_ref, o0_ref), (w1_ref, o1_ref), (w2_ref, o2_ref))):
        @pl.when(j // INPROJ_SPLIT == sec)
        def _(w_ref=w_ref, o_ref=o_ref):
            o_ref[...] = _dot(h_scr[...], w_ref[...])


def _inproj(x, mod, which0, n_pre, w_secs, *, tm, tiles_per_group):
    rows = x.shape[0]

    def sec_col(sec):
        return lambda i, j: jnp.clip(j - sec * INPROJ_SPLIT, 0, INPROJ_SPLIT - 1)

    w_specs, o_specs = [], []
    for sec, width in enumerate(INPROJ_WIDTHS):
        tn = width // INPROJ_SPLIT
        col = sec_col(sec)
        w_specs.append(pl.BlockSpec((D_MODEL, tn), lambda i, j, col=col: (0, col(i, j))))
        o_specs.append(pl.BlockSpec((tm, tn), lambda i, j, col=col: (i, col(i, j))))
    return pl.pallas_call(
        _inproj_kernel,
        out_shape=tuple(jax.ShapeDtypeStruct((rows, w), F32) for w in INPROJ_WIDTHS),
        grid=(rows // tm, len(INPROJ_WIDTHS) * INPROJ_SPLIT),
        in_specs=[
            pl.BlockSpec((tm, D_MODEL), lambda i, j: (i, 0)),
            _mod_spec(mod, which0, tm, tiles_per_group),
            _mod_spec(mod, which0 + 1, tm, tiles_per_group),
            _vec_spec(),
            *w_specs,
        ],
        out_specs=tuple(o_specs),
        scratch_shapes=[pltpu.VMEM((tm, D_MODEL), BF16)],
        compiler_params=_cparams("parallel", "arbitrary"),
        name="inproj",
    )(x, mod, mod, n_pre, *w_secs)


def _split_w_in(w_in):
    w = w_in.astype(BF16)
    c1 = SHIFT_W
    c2 = SHIFT_W + 3 * D_MODEL
    return w[:, :c1], w[:, c1:c2], w[:, c2:]


def _rope_tables(pos):
    half = HEAD_DIM // 2
    inv = ROPE_THETA ** (-jnp.arange(half, dtype=F32) / half)
    ang = pos.astype(F32)[:, None] * inv[None, :]
    cos, sin = jnp.cos(ang), jnp.sin(ang)
    return jnp.tile(cos, (1, 4)), jnp.tile(jnp.concatenate([-sin, sin], axis=1), (1, 2))


def _rope_pair(x, cos, sin_signed):
    lane = lax.broadcasted_iota(jnp.int32, x.shape, 1)
    first_half = (lane % HEAD_DIM) < HEAD_DIM // 2
    partner = jnp.where(first_half, pltpu.roll(x, PAIR_W - HEAD_DIM // 2, 1), pltpu.roll(x, HEAD_DIM // 2, 1))
    return x * cos + partner * sin_signed


def _topk_bias(gate, n_past):
    nb = gate.shape[0]
    blk = lax.broadcasted_iota(jnp.int32, gate.shape, 0)
    cnt = jnp.zeros(gate.shape, jnp.int32)
    for jp in range(nb):
        gj = gate[jp:jp + 1, :]
        beats = (gj > gate) | ((gj == gate) & (jp < blk))
        cnt = cnt + jnp.where(beats & (jp < n_past), 1, 0)
    sel = (blk < n_past) & (cnt < MOBA_TOPK)
    return jnp.where(sel, 0.0, NEG_INF)


def _moba_prompt_kernel(q_ref, k_ref, v_ref, cos_ref, sin_ref, o_ref, ko_ref, vo_ref,
                        kbf_scr, vt_scr, kmean_scr, bias_scr):
    qi = pl.program_id(2)
    nb = kbf_scr.shape[0]
    blk = MOBA_BLOCK

    @pl.when(qi == 0)
    def _():
        def prep(j, carry):
            rows = pl.ds(pl.multiple_of(j * blk, blk), blk)
            kr = _rope_pair(k_ref[rows, :], cos_ref[rows, :], sin_ref[rows, :])
            ko_ref[rows, :] = kr
            kbf_scr[j] = kr.astype(BF16)
            kmean_scr[pl.ds(j, 1), :] = jnp.mean(kr, axis=0, keepdims=True)
            v = v_ref[rows, :]
            vo_ref[rows, :] = v
            vt_scr[j] = v.T.astype(BF16)
            return carry

        lax.fori_loop(0, nb, prep, 0)

    rows_q = pl.ds(pl.multiple_of(qi * blk, blk), blk)
    q = _rope_pair(q_ref[...], cos_ref[rows_q, :], sin_ref[rows_q, :])
    lane = lax.broadcasted_iota(jnp.int32, q.shape, 1)
    kmean = kmean_scr[...]
    kidx = lax.broadcasted_iota(jnp.int32, (blk, blk), 0)
    qidx = lax.broadcasted_iota(jnp.int32, (blk, blk), 1)
    outs = []
    for h in range(2):
        qh = jnp.where(lane // HEAD_DIM == h, q, 0.0)
        gate = lax.dot_general(kmean, qh, (((1,), (1,)), ((), ())), precision=lax.Precision.HIGHEST,
                               preferred_element_type=F32)
        bias_scr[h] = _topk_bias(gate, qi)
        qs = (qh * HEAD_DIM ** -0.5).astype(BF16)
        hrows = slice(h * HEAD_DIM, (h + 1) * HEAD_DIM)

        s = jnp.where(kidx <= qidx, _dot_nt(kbf_scr[qi], qs), NEG_INF)
        m = jnp.max(s, axis=0, keepdims=True)
        p = jnp.exp(s - m)
        l = jnp.sum(p, axis=0, keepdims=True)
        acc = _dot(vt_scr[qi, hrows, :], p.astype(BF16))

        def step(j, carry, h=h, qs=qs, hrows=hrows):
            m, l, acc = carry
            s = _dot_nt(kbf_scr[j], qs) + bias_scr[h, pl.ds(j, 1), :]
            m_new = jnp.maximum(m, jnp.max(s, axis=0, keepdims=True))
            alpha = jnp.exp(m - m_new)
            p = jnp.exp(s - m_new)
            l = alpha * l + jnp.sum(p, axis=0, keepdims=True)
            acc = alpha * acc + _dot(vt_scr[j, hrows, :], p.astype(BF16))
            return m_new, l, acc

        m, l, acc = lax.fori_loop(0, qi, step, (m, l, acc))
        outs.append(acc / l)
    o_ref[...] = jnp.concatenate(outs, axis=0).T.astype(o_ref.dtype)


def _moba_prompt(qkv, cos_t, sin_t, n_seq, seq_len):
    proj3 = qkv.reshape(n_seq, seq_len, 3 * D_MODEL)
    nb = seq_len // MOBA_BLOCK
    qb, kb, vb = 0, N_PAIRS, 2 * N_PAIRS
    seq_spec = lambda col0: pl.BlockSpec((None, seq_len, PAIR_W), lambda b, hp, qi: (b, 0, col0 + hp))
    tab_spec = pl.BlockSpec((seq_len, PAIR_W), lambda b, hp, qi: (0, 0))
    return pl.pallas_call(
        _moba_prompt_kernel,
        out_shape=(
            jax.ShapeDtypeStruct((n_seq, seq_len, D_MODEL), BF16),
            jax.ShapeDtypeStruct((n_seq, seq_len, D_MODEL), F32),
            jax.ShapeDtypeStruct((n_seq, seq_len, D_MODEL), F32),
        ),
        grid=(n_seq, N_PAIRS, nb),
        in_specs=[
            pl.BlockSpec((None, MOBA_BLOCK, PAIR_W), lambda b, hp, qi: (b, qi, qb + hp)),
            seq_spec(kb),
            seq_spec(vb),
            tab_spec,
            tab_spec,
        ],
        out_specs=(
            pl.BlockSpec((None, MOBA_BLOCK, PAIR_W), lambda b, hp, qi: (b, qi, hp)),
            seq_spec(0),
            seq_spec(0),
        ),
        scratch_shapes=[
            pltpu.VMEM((nb, MOBA_BLOCK, PAIR_W), BF16),
            pltpu.VMEM((nb, PAIR_W, MOBA_BLOCK), BF16),
            pltpu.VMEM((nb, PAIR_W), F32),
            pltpu.VMEM((2, nb, MOBA_BLOCK), F32),
        ],
        compiler_params=_cparams("parallel", "parallel", "arbitrary"),
        name="moba_prompt",
    )(proj3, proj3, proj3, cos_t, sin_t)


def _hi_lo(x):
    hi = x.astype(BF16)
    return hi, (x - hi.astype(F32)).astype(BF16)


def _mm3(dot_fn, a, b):
    ah, al = _hi_lo(a)
    bh, bl = _hi_lo(b)
    return dot_fn(ah, bh) + dot_fn(ah, bl) + dot_fn(al, bh)


def _mm1(dot_fn, a, b):
    return dot_fn(a.astype(BF16), b.astype(BF16))


_INV_MM = _mm1


def _head_ones3():
    r = lax.broadcasted_iota(jnp.int32, (3 * PAIR_W, PAIR_W), 0)
    c = lax.broadcasted_iota(jnp.int32, (3 * PAIR_W, PAIR_W), 1)
    return jnp.where((r % PAIR_W) // HEAD_DIM == c // HEAD_DIM, 1.0, 0.0).astype(BF16)


def _head_sums(x, ones3):
    hi = x.astype(BF16)
    r1 = x - hi.astype(F32)
    mid = r1.astype(BF16)
    lo = (r1 - mid.astype(F32)).astype(BF16)
    return _dot(jnp.concatenate([hi, mid, lo], axis=1), ones3)


LORA_W = DECAY_LORA + AAA_LORA


def _rwkv_prep_kernel(p_ref, ovr_ref, mu_ref, w0_ref, w2_ref, a0_ref, a2_ref, g2_ref, kk_ref, ka_ref, rk_ref,
                      r_o, lw_o, k_o, v_o, kap_o, b_o, g_o, bonus_o, *, period):
    p = p_ref[...]
    tm = p.shape[0]
    row = lax.broadcasted_iota(jnp.int32, (tm, 1), 0)
    prev = jnp.where(row % period == 0, ovr_ref[...], pltpu.roll(p, 1, 0))
    xm = p + (prev - p) * mu_ref[...]
    r = xm[:, :D_MODEL]
    k = xm[:, D_MODEL:2 * D_MODEL]
    v = xm[:, 2 * D_MODEL:3 * D_MODEL]
    lora = xm[:, 3 * D_MODEL:3 * D_MODEL + LORA_W]
    xg = xm[:, 3 * D_MODEL + LORA_W:]
    z = w0_ref[...] + jnp.dot(jnp.tanh(lora), w2_ref[...], precision=lax.Precision.HIGHEST,
                              preferred_element_type=F32)
    w_log = -jax.nn.softplus(-z) - 0.5
    a = jax.nn.sigmoid(a0_ref[...] + _dot(lora.astype(BF16), a2_ref[...]))
    g_o[...] = _dot(jax.nn.sigmoid(xg).astype(BF16), g2_ref[...])
    kk = k * kk_ref[...]
    k2 = k * (1.0 + (a - 1.0) * ka_ref[...])
    rk = r * k2 * rk_ref[...]
    lw = -jnp.exp(w_log)
    ones3 = _head_ones3()
    for pr in range(N_PAIRS):
        sl = slice(pr * PAIR_W, (pr + 1) * PAIR_W)
        kks = kk[:, sl]
        kap = kks / jnp.maximum(jnp.sqrt(_head_sums(kks * kks, ones3)), 1e-12)
        r_o[pr] = r[:, sl]
        lw_o[pr] = lw[:, sl]
        k_o[pr] = k2[:, sl]
        v_o[pr] = v[:, sl]
        kap_o[pr] = kap
        b_o[pr] = kap * a[:, sl]
        bonus_o[:, sl] = _head_sums(rk[:, sl], ones3) * v[:, sl]


def _rwkv_prep(p_rwkv, ovr, rw, *, tm, period):
    rows = p_rwkv.shape[0]
    if ovr.shape[1] == 1:
        ovr_spec = pl.BlockSpec((None, 1, SHIFT_W), lambda i: (i, 0, 0))
    else:
        ovr_spec = pl.BlockSpec((None, tm, SHIFT_W), lambda i: (0, i, 0))
    full = lambda a: pl.BlockSpec(a.shape, lambda i: (0,) * a.ndim)
    pm = jax.ShapeDtypeStruct((N_PAIRS, rows, PAIR_W), F32)
    rm = jax.ShapeDtypeStruct((rows, D_MODEL), F32)
    pm_spec = pl.BlockSpec((N_PAIRS, tm, PAIR_W), lambda i: (0, i, 0))
    rm_spec = pl.BlockSpec((tm, D_MODEL), lambda i: (i, 0))
    consts = (rw["mu"], rw["w0"], rw["w2"], rw["a0"], rw["a2"], rw["g2"], rw["k_k"], rw["k_a"], rw["r_k"])
    return pl.pallas_call(
        functools.partial(_rwkv_prep_kernel, period=period),
        out_shape=(pm,) * 6 + (rm, rm),
        grid=(rows // tm,),
        in_specs=[pl.BlockSpec((tm, SHIFT_W), lambda i: (i, 0)), ovr_spec] + [full(a) for a in consts],
        out_specs=(pm_spec,) * 6 + (rm_spec, rm_spec),
        compiler_params=_cparams("parallel"),
        name="rwkv_prep",
    )(p_rwkv, ovr, *consts)


def _rwkv_weights(mu_shift, w0, w2_decay, a0, a2, g2, k_k, k_a, r_k):
    zeros = jnp.zeros((DECAY_LORA, D_MODEL), F32)
    return dict(
        mu=mu_shift.reshape(1, SHIFT_W), w0=w0.reshape(1, D_MODEL), a0=a0.reshape(1, D_MODEL),
        w2=jnp.concatenate([w2_decay, zeros], axis=0),
        a2=jnp.concatenate([zeros, a2], axis=0).astype(BF16),
        g2=g2.astype(BF16), k_k=k_k.reshape(1, D_MODEL), k_a=k_a.reshape(1, D_MODEL),
        r_k=r_k.reshape(1, D_MODEL))


INV_BASE = 8


def _unit_lower_inverse(l_m, ti, tj):
    cl = l_m.shape[0]
    base = min(INV_BASE, cl)
    l_b = jnp.where(ti // base == tj // base, l_m, 0.0)
    t_m = jnp.where(ti == tj, 1.0, 0.0) - l_b
    for _ in range(base.bit_length() - 2):
        l_b = _INV_MM(_dot, l_b, l_b)
        t_m = t_m + _INV_MM(_dot, t_m, l_b)
    size = base
    while size < cl:
        off = jnp.where((ti // (2 * size) == tj // (2 * size)) & (ti // size != tj // size), l_m, 0.0)
        t_m = t_m - _INV_MM(_dot, _INV_MM(_dot, t_m, off), t_m)
        size *= 2
    return t_m


def _rwkv_chunk_kernel(r_ref, lw_ref, k_ref, v_ref, kap_ref, b_ref, s0_ref, o_ref, so_ref, st_scr, *, unroll):
    c = pl.program_id(1)
    n_c = pl.num_programs(1)
    cl = r_ref.shape[1]
    hd = HEAD_DIM

    @pl.when(c == 0)
    def _():
        z = jnp.zeros((hd, hd), F32)
        for pr in range(N_PAIRS):
            top = jnp.concatenate([s0_ref[2 * pr], z], axis=1)
            bot = jnp.concatenate([z, s0_ref[2 * pr + 1]], axis=1)
            st_scr[pr] = jnp.concatenate([top, bot], axis=0)

    ti = lax.broadcasted_iota(jnp.int32, (cl, cl), 0)
    tj = lax.broadcasted_iota(jnp.int32, (cl, cl), 1)
    tril = ti >= tj
    stril = ti > tj
    tril_f = jnp.where(tril, 1.0, 0.0)
    head1 = lax.broadcasted_iota(jnp.int32, (cl, PAIR_W), 1) >= hd
    sr = lax.broadcasted_iota(jnp.int32, (PAIR_W, PAIR_W), 0)
    sc = lax.broadcasted_iota(jnp.int32, (PAIR_W, PAIR_W), 1)
    same_head = (sr // hd) == (sc // hd)

    def pair_body(pr, carry):
        lw = lw_ref[pr]
        r = r_ref[pr]
        k = k_ref[pr]
        v = v_ref[pr]
        kap = kap_ref[pr]
        b = b_ref[pr]
        cum = jnp.dot(tril_f, lw, precision=lax.Precision.HIGHEST, preferred_element_type=F32)
        tot = cum[cl - 1:cl, :]
        g_inv = jnp.exp(-cum)
        g_end = jnp.exp(tot - cum)
        kap_t = kap * jnp.exp(cum - lw)
        r_t = r * jnp.exp(cum)
        kt_b = (k * g_inv).astype(BF16)
        bt_b = (b * g_inv).astype(BF16)
        kh_b = (k * g_end).astype(BF16)
        bh_b = (b * g_end).astype(BF16)
        v_b = v.astype(BF16)
        ws, us, rps, ois = [], [], [], []
        for h in range(2):
            mh = head1 if h else jnp.logical_not(head1)
            kap_h = jnp.where(mh, kap_t, 0.0)
            r_h = jnp.where(mh, r_t, 0.0)
            kap_hb = kap_h.astype(BF16)
            r_hb = r_h.astype(BF16)
            l_m = jnp.where(stril, _dot_nt(kap_hb, bt_b), 0.0)
            a_kk = jnp.where(stril, _dot_nt(kap_hb, kt_b), 0.0)
            a_rk = jnp.where(tril, _dot_nt(r_hb, kt_b), 0.0)
            a_rb = jnp.where(tril, _dot_nt(r_hb, bt_b), 0.0)
            t_m = _unit_lower_inverse(l_m, ti, tj)
            t_b = t_m.astype(BF16)
            a_rb_b = a_rb.astype(BF16)
            w_h = _dot(t_b, kap_hb)
            u_h = _dot(t_b, _dot(a_kk.astype(BF16), v_b).astype(BF16))
            ws.append(w_h)
            us.append(u_h)
            rps.append(r_h - _dot(a_rb_b, w_h.astype(BF16)))
            ois.append(_dot(a_rk.astype(BF16), v_b) - _dot(a_rb_b, u_h.astype(BF16)))
        w_m = ws[0] + ws[1]
        u_m = jnp.where(head1, us[1], us[0])
        rp = rps[0] + rps[1]
        oi = jnp.where(head1, ois[1], ois[0])
        s = st_scr[pr]
        o_ref[pr] = _mm3(_dot_nt, rp, s) + oi
        wtb = jnp.where(same_head, _dot_tn(w_m.astype(BF16), bh_b), 0.0)
        n_t = jnp.where(same_head, _dot_tn(v_b, kh_b) - _dot_tn(u_m.astype(BF16), bh_b), 0.0)
        st_scr[pr] = s * jnp.exp(tot) - _mm3(_dot, s, wtb) + n_t
        return carry

    lax.fori_loop(0, N_PAIRS, pair_body, 0, unroll=unroll)

    @pl.when(c == n_c - 1)
    def _():
        for pr in range(N_PAIRS):
            s = st_scr[pr]
            so_ref[2 * pr] = s[:hd, :hd]
            so_ref[2 * pr + 1] = s[hd:, hd:]


def _rwkv_chunks(pm_inputs, s0, *, n_seq, chunk, unroll):
    rows = pm_inputs[0].shape[1]
    n_c = rows // n_seq // chunk
    pm_spec = pl.BlockSpec((N_PAIRS, chunk, PAIR_W), lambda s, c: (0, s * n_c + c, 0))
    st_spec = pl.BlockSpec((None, N_HEADS, HEAD_DIM, HEAD_DIM), lambda s, c: (s, 0, 0, 0))
    return pl.pallas_call(
        functools.partial(_rwkv_chunk_kernel, unroll=unroll),
        out_shape=(jax.ShapeDtypeStruct((N_PAIRS, rows, PAIR_W), F32),
                   jax.ShapeDtypeStruct((n_seq, N_HEADS, HEAD_DIM, HEAD_DIM), F32)),
        grid=(n_seq, n_c),
        in_specs=[pm_spec] * 6 + [st_spec],
        out_specs=(pm_spec, st_spec),
        scratch_shapes=[pltpu.VMEM((N_PAIRS, PAIR_W, PAIR_W), F32)],
        compiler_params=_cparams("parallel", "arbitrary"),
        name="rwkv_chunks",
    )(*pm_inputs, s0)


def _mixer_out_kernel(x_ref, gt_ref, npost_ref, o_ref, bonus_ref, g_ref, attn_ref, ga_ref, gb_ref,
                      lnw_ref, lnb_ref, wor_ref, wom_ref, wout_ref, y_ref):
    ones3 = _head_ones3()
    parts = []
    for pr in range(N_PAIRS):
        o = o_ref[pr]
        mu = _head_sums(o, ones3) * (1.0 / HEAD_DIM)
        d = o - mu
        var = _head_sums(d * d, ones3) * (1.0 / HEAD_DIM)
        parts.append(d * lax.rsqrt(var + LNX_EPS))
    on = jnp.concatenate(parts, axis=1) * lnw_ref[...] + lnb_ref[...]
    ya = _dot(((on + bonus_ref[...]) * g_ref[...]).astype(BF16), wor_ref[...])
    yb = _dot(attn_ref[...], wom_ref[...])
    y = jax.nn.sigmoid(ga_ref[...]) * ya + jax.nn.sigmoid(gb_ref[...]) * yb
    y = _dot(y.astype(BF16), wout_ref[...])
    y_ref[...] = x_ref[...] + gt_ref[...] * _rms(y, npost_ref[...])


def _mixer_out(x, mod, which, n_post, o_pm, bonus, g, attn, gates, lnx_w, lnx_b, w_o_rwkv, w_o_moba, w_out,
               *, tm, tiles_per_group):
    rows = x.shape[0]
    row_spec = pl.BlockSpec((tm, D_MODEL), lambda i: (i, 0))
    w_spec = pl.BlockSpec((D_MODEL, D_MODEL), lambda i: (0, 0))
    return pl.pallas_call(
        _mixer_out_kernel,
        out_shape=jax.ShapeDtypeStruct((rows, D_MODEL), F32),
        grid=(rows // tm,),
        in_specs=[
            row_spec,
            _mod_spec(mod, which, tm, tiles_per_group),
            _vec_spec(),
            pl.BlockSpec((N_PAIRS, tm, PAIR_W), lambda i: (0, i, 0)),
            row_spec, row_spec, row_spec,
            pl.BlockSpec((tm, D_MODEL), lambda i: (i, 0)),
            pl.BlockSpec((tm, D_MODEL), lambda i: (i, 1)),
            _vec_spec(), _vec_spec(), w_spec, w_spec, w_spec,
        ],
        out_specs=row_spec,
        compiler_params=_cparams("parallel"),
        name="mixer_out",
    )(x, mod, n_post, o_pm, bonus, g, attn, gates, gates, lnx_w, lnx_b, w_o_rwkv, w_o_moba, w_out)


def _rope_rows_kernel(qkv_ref, cos_ref, sin_ref, q_ref, k_ref):
    cos = cos_ref[...]
    sin = sin_ref[...]
    for pr in range(N_PAIRS):
        sl = slice(pr * PAIR_W, (pr + 1) * PAIR_W)
        q_ref[:, sl] = _rope_pair(qkv_ref[:, sl], cos, sin)
        k_ref[:, sl] = _rope_pair(qkv_ref[:, D_MODEL + pr * PAIR_W:D_MODEL + (pr + 1) * PAIR_W], cos, sin)


def _rope_rows(qkv, cos_t, sin_t):
    rows = qkv.shape[0]
    out = jax.ShapeDtypeStruct((rows, D_MODEL), F32)
    return pl.pallas_call(_rope_rows_kernel, out_shape=(out, out), name="rope_rows")(qkv, cos_t, sin_t)


TOK_SLOTS = 8


def _moba_sample_kernel(pt_ref, q_ref, kn_ref, vn_ref, kp_ref, vp_ref, o_ref,
                        qh_scr, ql_scr, s_scr, acc_scr, l_scr, *, n_pages, n_tok):
    j = pl.program_id(1)
    rows = N_HEADS * TOK_SLOTS
    n_blk = n_pages * PAGE_SIZE // MOBA_BLOCK
    scale = HEAD_DIM ** -0.5
    row_head = lax.broadcasted_iota(jnp.int32, (rows, D_MODEL), 0) // TOK_SLOTS
    lane_head = lax.broadcasted_iota(jnp.int32, (rows, D_MODEL), 1) // HEAD_DIM

    @pl.when(j == 0)
    def _():
        q_bd = jnp.where(row_head == lane_head, jnp.tile(q_ref[...], (N_HEADS, 1)), 0.0)
        hi, lo = _hi_lo(q_bd)
        qh_scr[...] = hi
        ql_scr[...] = lo

    @pl.when(j < n_pages)
    def _():
        kh, kl = _hi_lo(kp_ref[...])
        qh = qh_scr[...]
        s_scr[j] = _dot(qh, kh) + _dot(ql_scr[...], kh) + _dot(qh, kl)

    @pl.when(j == n_pages - 1)
    def _():
        cols = [jnp.sum(s_scr[2 * n] + s_scr[2 * n + 1], axis=1, keepdims=True) for n in range(n_blk)]
        gate = jnp.concatenate(cols, axis=1)
        col = lax.broadcasted_iota(jnp.int32, gate.shape, 1)
        cnt = jnp.zeros(gate.shape, jnp.int32)
        for n in range(n_blk):
            gn = gate[:, n:n + 1]
            cnt = cnt + jnp.where((gn > gate) | ((gn == gate) & (n < col)), 1, 0)
        bias = jnp.where(cnt < MOBA_TOPK, 0.0, NEG_INF)
        knh, knl = _hi_lo(kn_ref[...])
        qh = qh_scr[...]
        s_new = (_dot_nt(qh, knh) + _dot_nt(ql_scr[...], knh) + _dot_nt(qh, knl)) * scale
        t_q = lax.broadcasted_iota(jnp.int32, s_new.shape, 0) % TOK_SLOTS
        t_k = lax.broadcasted_iota(jnp.int32, s_new.shape, 1)
        s_new = jnp.where((t_k <= t_q) & (t_k < n_tok), s_new, NEG_INF)
        mx = None
        for pg in range(n_pages):
            t = s_scr[pg] * scale + bias[:, pg // 2:pg // 2 + 1]
            mx = t if mx is None else jnp.maximum(mx, t)
        m = jnp.maximum(jnp.max(mx, axis=1, keepdims=True), jnp.max(s_new, axis=1, keepdims=True))
        p_new = jnp.exp(s_new - m)
        tot = None
        for pg in range(n_pages):
            p = jnp.exp(s_scr[pg] * scale + bias[:, pg // 2:pg // 2 + 1] - m)
            s_scr[pg] = p
            tot = p if tot is None else tot + p
        l_scr[...] = jnp.sum(tot, axis=1, keepdims=True) + jnp.sum(p_new, axis=1, keepdims=True)
        acc_scr[...] = _dot(p_new, vn_ref[...])

    @pl.when(j >= n_pages)
    def _():
        acc_scr[...] += _dot_nt(s_scr[j - n_pages].astype(BF16), vp_ref[...].astype(BF16))

    @pl.when(j == 2 * n_pages - 1)
    def _():
        acc = jnp.where(row_head == lane_head, acc_scr[...] / l_scr[...], 0.0)
        out = acc[:TOK_SLOTS]
        for h in range(1, N_HEADS):
            out = out + acc[h * TOK_SLOTS:(h + 1) * TOK_SLOTS]
        o_ref[...] = out


def _moba_sample(page_table, q_rot, k_new, v_new, cache_kt, cache_vt, *, n_tok):
    n_seq, n_pages = page_table.shape
    assert n_pages % 2 == 0 and n_tok <= TOK_SLOTS
    rows = N_HEADS * TOK_SLOTS
    tok_spec = pl.BlockSpec((None, TOK_SLOTS, D_MODEL), lambda b, j, pt: (b, 0, 0))
    kp_spec = pl.BlockSpec((None, D_MODEL, PAGE_SIZE),
                           lambda b, j, pt: (pt[b * n_pages + jnp.minimum(j, n_pages - 1)], 0, 0))
    vp_spec = pl.BlockSpec((None, D_MODEL, PAGE_SIZE),
                           lambda b, j, pt: (pt[b * n_pages + jnp.maximum(j - n_pages, 0)], 0, 0))
    return pl.pallas_call(
        functools.partial(_moba_sample_kernel, n_pages=n_pages, n_tok=n_tok),
        out_shape=jax.ShapeDtypeStruct((n_seq, TOK_SLOTS, D_MODEL), F32),
        grid_spec=pltpu.PrefetchScalarGridSpec(
            num_scalar_prefetch=1,
            grid=(n_seq, 2 * n_pages),
            in_specs=[tok_spec, tok_spec, tok_spec, kp_spec, vp_spec],
            out_specs=tok_spec,
            scratch_shapes=[
                pltpu.VMEM((rows, D_MODEL), BF16),
                pltpu.VMEM((rows, D_MODEL), BF16),
                pltpu.VMEM((n_pages, rows, PAGE_SIZE), F32),
                pltpu.VMEM((rows, D_MODEL), F32),
                pltpu.VMEM((rows, 1), F32),
            ],
        ),
        compiler_params=_cparams("parallel", "arbitrary"),
        name="moba_sample",
    )(page_table.reshape(-1), q_rot, k_new, v_new, cache_kt, cache_vt)


DENSE_TM = 512
RWKV_TM = 256
PROMPT_CHUNK = 64
SAMPLE_CHUNK = 8


def _layer(x, mod, lw, *, tiles, shift_ovr, shift_period, attend, rwkv_scan):
    tpg_dense, tpg_rwkv = tiles
    x1 = _ffn(x, mod, 0, lw["n1_pre"], lw["n1_post"], lw["w1_gu"], lw["w1_down"], tm=DENSE_TM,
              tiles_per_group=tpg_dense)
    p_rwkv, qkv, gates = _inproj(x1, mod, 3, lw["n2_pre"], lw["w_in"], tm=DENSE_TM, tiles_per_group=tpg_dense)
    attn, extras = attend(qkv)
    prep = _rwkv_prep(p_rwkv, shift_ovr(p_rwkv), lw["rwkv"], tm=RWKV_TM, period=shift_period)
    o_pm, state = rwkv_scan(prep[:6])
    x2 = _mixer_out(x1, mod, 5, lw["n2_post"], o_pm, prep[7], prep[6], attn, gates, lw["lnx_w"], lw["lnx_b"],
                    lw["w_o_rwkv"], lw["w_o_moba"], lw["w_out"], tm=RWKV_TM, tiles_per_group=tpg_rwkv)
    y = _ffn(x2, mod, 6, lw["n3_pre"], lw["n3_post"], lw["w3_gu"], lw["w3_down"], tm=DENSE_TM,
             tiles_per_group=tpg_dense)
    return y, p_rwkv, qkv, state, extras


def kernel(x_prompt, x_sample, cache_k, cache_v, state_rwkv, state_shift, page_table, c_prompt, c_sample, w_ada, b_ada, n1_pre, n1_post, w1_gu, w1_down, n2_pre, n2_post, w_in, mu_shift, w0, w2_decay, a0, a2, g2, k_k, k_a, r_k, lnx_w, lnx_b, w_o_rwkv, w_o_moba, w_out, n3_pre, n3_post, w3_gu, w3_down):
    assert w_ada.shape[0] == 1, "single layer"
    bp, seq, _ = x_prompt.shape
    bs, ts, _ = x_sample.shape
    n_pool = cache_k.shape[1]
    n_pages = page_table.shape[1]
    rows_p, rows_s = bp * seq, bs * ts
    assert seq % DENSE_TM == 0 and seq % MOBA_BLOCK == 0 and rows_s % DENSE_TM == 0 and ts <= SAMPLE_CHUNK

    lw = dict(
        n1_pre=n1_pre, n1_post=n1_post, n2_pre=n2_pre, n2_post=n2_post, n3_pre=n3_pre, n3_post=n3_post,
        w1_gu=w1_gu[0].astype(BF16), w1_down=w1_down[0].astype(BF16),
        w3_gu=w3_gu[0].astype(BF16), w3_down=w3_down[0].astype(BF16),
        w_in=_split_w_in(w_in[0]),
        rwkv=_rwkv_weights(mu_shift[0], w0[0], w2_decay[0], a0[0], a2[0], g2[0], k_k[0], k_a[0], r_k[0]),
        lnx_w=lnx_w, lnx_b=lnx_b,
        w_o_rwkv=w_o_rwkv[0].astype(BF16), w_o_moba=w_o_moba[0].astype(BF16), w_out=w_out[0].astype(BF16),
    )

    n_c = bp + bs
    c_all = jnp.concatenate([c_prompt, c_sample, jnp.zeros((-n_c % 8, D_MODEL), F32)], axis=0)
    mod = _ada(c_all, w_ada[0], b_ada[0])
    mod_p = mod[:bp].reshape(bp, N_ADA, 1, D_MODEL).transpose(1, 0, 2, 3)
    mod_s = jnp.repeat(mod[bp:n_c].reshape(bs, N_ADA, D_MODEL), ts, axis=0).transpose(1, 0, 2)[:, None]

    cos_p, sin_p = _rope_tables(jnp.arange(seq, dtype=jnp.int32))

    def attend_p(qkv):
        attn, k_rot, v = _moba_prompt(qkv, cos_p, sin_p, bp, seq)
        return attn.reshape(rows_p, D_MODEL), (k_rot, v)

    def shift_ovr_p(p_rwkv):
        tiles_per_seq = seq // RWKV_TM
        prev = jnp.concatenate([jnp.zeros((1, SHIFT_W), F32), p_rwkv[RWKV_TM - 1::RWKV_TM][:-1]], axis=0)
        first = (jnp.arange(rows_p // RWKV_TM) % tiles_per_seq == 0)[:, None]
        return jnp.where(first, 0.0, prev)[:, None, :]

    def scan_p(pm):
        s0 = jnp.zeros((bp, N_HEADS, HEAD_DIM, HEAD_DIM), F32)
        return _rwkv_chunks(pm, s0, n_seq=bp, chunk=PROMPT_CHUNK, unroll=2)

    y_p, p_p, _, state_p, (k_p, v_p) = _layer(
        x_prompt.reshape(rows_p, D_MODEL), mod_p, lw, tiles=(seq // DENSE_TM, seq // RWKV_TM),
        shift_ovr=shift_ovr_p, shift_period=RWKV_TM, attend=attend_p, rwkv_scan=scan_p)

    pos_s = n_pages * PAGE_SIZE + jnp.arange(ts, dtype=jnp.int32)
    cos_s, sin_s = (jnp.tile(t, (bs, 1)) for t in _rope_tables(pos_s))
    pad_tok = lambda a: jnp.pad(a.reshape(bs, ts, D_MODEL), ((0, 0), (0, TOK_SLOTS - ts), (0, 0)))
    cache_kt = cache_k[0].transpose(0, 2, 3, 1).reshape(n_pool, D_MODEL, PAGE_SIZE)
    cache_vt = cache_v[0].transpose(0, 2, 3, 1).reshape(n_pool, D_MODEL, PAGE_SIZE)

    def attend_s(qkv):
        q_rot, k_rot = _rope_rows(qkv, cos_s, sin_s)
        v = qkv[:, 2 * D_MODEL:]
        attn = _moba_sample(page_table, pad_tok(q_rot), pad_tok(k_rot), pad_tok(v), cache_kt, cache_vt, n_tok=ts)
        return attn[:, :ts].reshape(rows_s, D_MODEL).astype(BF16), (k_rot, v)

    def scan_s(pm):
        pad = lambda a: jnp.pad(a.reshape(N_PAIRS, bs, ts, PAIR_W),
                                ((0, 0), (0, 0), (0, SAMPLE_CHUNK - ts), (0, 0))).reshape(N_PAIRS, -1, PAIR_W)
        o_pm, state = _rwkv_chunks([pad(a) for a in pm], state_rwkv[0], n_seq=bs, chunk=SAMPLE_CHUNK,
                                   unroll=N_PAIRS)
        o_pm = o_pm.reshape(N_PAIRS, bs, SAMPLE_CHUNK, PAIR_W)[:, :, :ts].reshape(N_PAIRS, rows_s, PAIR_W)
        return o_pm, state

    y_s, p_s, _, state_s, (k_s, v_s) = _layer(
        x_sample.reshape(rows_s, D_MODEL), mod_s, lw, tiles=(1, 1),
        shift_ovr=lambda p: jnp.repeat(state_shift[0], ts, axis=0)[None], shift_period=ts,
        attend=attend_s, rwkv_scan=scan_s)

    heads = lambda a, b, t: a.reshape(1, b, t, N_HEADS, HEAD_DIM)
    return (
        y_p.reshape(bp, seq, D_MODEL),
        y_s.reshape(bs, ts, D_MODEL),
        heads(k_p, bp, seq), heads(v_p, bp, seq),
        state_p[None], p_p.reshape(bp, seq, SHIFT_W)[:, -1][None],
        heads(k_s, bs, ts), heads(v_s, bs, ts),
        state_s[None], p_s.reshape(bs, ts, SHIFT_W)[:, -1][None],
    )
```

```python
import functools

import jax
import jax.numpy as jnp
from jax import lax
from jax.experimental import pallas as pl
from jax.experimental.pallas import tpu as pltpu

D_MODEL = 1024
HEAD_DIM = 64
N_HEADS = D_MODEL // HEAD_DIM
PAIR_W = 2 * HEAD_DIM
N_PAIRS = N_HEADS // 2
DECAY_LORA = 64
AAA_LORA = 64
GATE_LORA = 128
SHIFT_W = 3 * D_MODEL + DECAY_LORA + AAA_LORA + GATE_LORA
IN_W = SHIFT_W + 3 * D_MODEL + 2 * D_MODEL
Q_COL = SHIFT_W
K_COL = SHIFT_W + D_MODEL
V_COL = SHIFT_W + 2 * D_MODEL
GA_COL = SHIFT_W + 3 * D_MODEL
GB_COL = SHIFT_W + 4 * D_MODEL
D_FF = 11 * D_MODEL // 4
N_ADA = 9
MACARON_W = 0.5
RMS_EPS = 1e-6
LNX_EPS = 64e-5
MOBA_BLOCK = 256
MOBA_TOPK = 3
PAGE_SIZE = 128
ROPE_THETA = 10000.0

V7X_LANES = 128
VMEM_LIMIT = 56 * 1024 * 1024

BF16 = jnp.bfloat16
F32 = jnp.float32
NEG_INF = float("-inf")


def _cparams(*sem):
    return pltpu.CompilerParams(dimension_semantics=sem, vmem_limit_bytes=VMEM_LIMIT)


def _rms(x, g):
    return x * lax.rsqrt(jnp.mean(x * x, axis=-1, keepdims=True) + RMS_EPS) * g


def _dot(a, b):
    return jnp.dot(a, b, preferred_element_type=F32)


def _dot_nt(a, b):
    return lax.dot_general(a, b, (((1,), (1,)), ((), ())), preferred_element_type=F32)


def _dot_tn(a, b):
    return lax.dot_general(a, b, (((0,), (0,)), ((), ())), preferred_element_type=F32)


ADA_TN = 1536


def _ada_kernel(c_ref, w_ref, b_ref, o_ref):
    c = c_ref[...]
    a = (c * jax.nn.sigmoid(c)).astype(BF16)
    o_ref[...] = _dot(a, w_ref[...].astype(BF16)) + b_ref[...]


def _ada(c_all, w_ada, b_ada):
    rows = c_all.shape[0]
    n_out = w_ada.shape[1]
    return pl.pallas_call(
        _ada_kernel,
        out_shape=jax.ShapeDtypeStruct((rows, n_out), F32),
        grid=(n_out // ADA_TN,),
        in_specs=[
            pl.BlockSpec((rows, D_MODEL), lambda j: (0, 0)),
            pl.BlockSpec((D_MODEL, ADA_TN), lambda j: (0, j)),
            pl.BlockSpec((1, ADA_TN), lambda j: (0, j)),
        ],
        out_specs=pl.BlockSpec((rows, ADA_TN), lambda j: (0, j)),
        compiler_params=_cparams("arbitrary"),
        name="ada_mod",
    )(c_all, w_ada, b_ada.reshape(1, n_out))


def _mod_spec(mod, which, tm, tiles_per_group):
    rows_in_group = mod.shape[2]
    if rows_in_group == 1:
        return pl.BlockSpec((None, None, 1, D_MODEL), lambda i, *_: (which, i // tiles_per_group, 0, 0))
    return pl.BlockSpec((None, None, tm, D_MODEL), lambda i, *_: (which, 0, i, 0))


def _vec_spec():
    return pl.BlockSpec((1, D_MODEL), lambda i, *_: (0, 0))


FFN_TF = D_FF // 2


def _ffn_kernel(x_ref, sh_ref, sc_ref, gt_ref, npre_ref, npost_ref, wg_ref, wu_ref, wd_ref, o_ref, h_scr, acc_scr):
    j = pl.program_id(1)

    @pl.when(j == 0)
    def _():
        h = _rms(x_ref[...], npre_ref[...]) * (1.0 + sc_ref[...]) + sh_ref[...]
        h_scr[...] = h.astype(BF16)
        acc_scr[...] = jnp.zeros_like(acc_scr)

    h = h_scr[...]
    g = _dot(h, wg_ref[...])
    u = _dot(h, wu_ref[...])
    a = (g * jax.nn.sigmoid(g) * u).astype(BF16)
    acc_scr[...] += _dot(a, wd_ref[...])

    @pl.when(j == pl.num_programs(1) - 1)
    def _():
        y = _rms(acc_scr[...], npost_ref[...])
        o_ref[...] = x_ref[...] + MACARON_W * gt_ref[...] * y


def _ffn(x, mod, which0, n_pre, n_post, wgu, wd, *, tm, tiles_per_group):
    rows = x.shape[0]
    nf = D_FF // FFN_TF
    row_spec = pl.BlockSpec((tm, D_MODEL), lambda i, j: (i, 0))
    return pl.pallas_call(
        _ffn_kernel,
        out_shape=jax.ShapeDtypeStruct((rows, D_MODEL), F32),
        grid=(rows // tm, nf),
        in_specs=[
            row_spec,
            _mod_spec(mod, which0, tm, tiles_per_group),
            _mod_spec(mod, which0 + 1, tm, tiles_per_group),
            _mod_spec(mod, which0 + 2, tm, tiles_per_group),
            _vec_spec(),
            _vec_spec(),
            pl.BlockSpec((D_MODEL, FFN_TF), lambda i, j: (0, j)),
            pl.BlockSpec((D_MODEL, FFN_TF), lambda i, j: (0, j + nf)),
            pl.BlockSpec((FFN_TF, D_MODEL), lambda i, j: (j, 0)),
        ],
        out_specs=row_spec,
        scratch_shapes=[pltpu.VMEM((tm, D_MODEL), BF16), pltpu.VMEM((tm, D_MODEL), F32)],
        compiler_params=_cparams("parallel", "arbitrary"),
        name="ffn",
    )(x, mod, mod, mod, n_pre, n_post, wgu, wgu, wd)


INPROJ_WIDTHS = (SHIFT_W, 3 * D_MODEL, 2 * D_MODEL)
INPROJ_SPLIT = 2


def _inproj_kernel(x_ref, sh_ref, sc_ref, npre_ref, w0_ref, w1_ref, w2_ref, o0_ref, o1_ref, o2_ref, h_scr):
    j = pl.program_id(1)

    @pl.when(j == 0)
    def _():
        h = _rms(x_ref[...], npre_ref[...]) * (1.0 + sc_ref[...]) + sh_ref[...]
        h_scr[...] = h.astype(BF16)

    for sec, (w_ref, o_ref) in enumerate(((w0_ref, o0_ref), (w1_ref, o1_ref), (w2_ref, o2_ref))):
        @pl.when(j // INPROJ_SPLIT == sec)
        def _(w_ref=w_ref, o_ref=o_ref):
            o_ref[...] = _dot(h_scr[...], w_ref[...])


def _inproj(x, mod, which0, n_pre, w_secs, *, tm, tiles_per_group):
    rows = x.shape[0]

    def sec_col(sec):
        return lambda i, j: jnp.clip(j - sec * INPROJ_SPLIT, 0, INPROJ_SPLIT - 1)

    w_specs, o_specs = [], []
    for sec, width in enumerate(INPROJ_WIDTHS):
        tn = width // INPROJ_SPLIT
        col = sec_col(sec)
        w_specs.append(pl.BlockSpec((D_MODEL, tn), lambda i, j, col=col: (0, col(i, j))))
        o_specs.append(pl.BlockSpec((tm, tn), lambda i, j, col=col: (i, col(i, j))))
    return pl.pallas_call(
        _inproj_kernel,
        out_shape=tuple(jax.ShapeDtypeStruct((rows, w), F32) for w in INPROJ_WIDTHS),
        grid=(rows // tm, len(INPROJ_WIDTHS) * INPROJ_SPLIT),
        in_specs=[
            pl.BlockSpec((tm, D_MODEL), lambda i, j: (i, 0)),
            _mod_spec(mod, which0, tm, tiles_per_group),
            _mod_spec(mod, which0 + 1, tm, tiles_per_group),
            _vec_spec(),
            *w_specs,
        ],
        out_specs=tuple(o_specs),
        scratch_shapes=[pltpu.VMEM((tm, D_MODEL), BF16)],
        compiler_params=_cparams("parallel", "arbitrary"),
        name="inproj",
    )(x, mod, mod, n_pre, *w_secs)


def _split_w_in(w_in):
    w = w_in.astype(BF16)
    c1 = SHIFT_W
    c2 = SHIFT_W + 3 * D_MODEL
    return w[:, :c1], w[:, c1:c2], w[:, c2:]


def _rope_tables(pos):
    half = HEAD_DIM // 2
    inv = ROPE_THETA ** (-jnp.arange(half, dtype=F32) / half)
    ang = pos.astype(F32)[:, None] * inv[None, :]
    cos, sin = jnp.cos(ang), jnp.sin(ang)
    return jnp.tile(cos, (1, 4)), jnp.tile(jnp.concatenate([-sin, sin], axis=1), (1, 2))


def _rope_pair(x, cos, sin_signed):
    lane = lax.broadcasted_iota(jnp.int32, x.shape, 1)
    first_half = (lane % HEAD_DIM) < HEAD_DIM // 2
    partner = jnp.where(first_half, pltpu.roll(x, PAIR_W - HEAD_DIM // 2, 1), pltpu.roll(x, HEAD_DIM // 2, 1))
    return x * cos + partner * sin_signed


def _topk_bias(gate, n_past):
    nb = gate.shape[0]
    blk = lax.broadcasted_iota(jnp.int32, gate.shape, 0)
    cnt = jnp.zeros(gate.shape, jnp.int32)
    for jp in range(nb):
        gj = gate[jp:jp + 1, :]
        beats = (gj > gate) | ((gj == gate) & (jp < blk))
        cnt = cnt + jnp.where(beats & (jp < n_past), 1, 0)
    sel = (blk < n_past) & (cnt < MOBA_TOPK)
    return jnp.where(sel, 0.0, NEG_INF)


def _moba_prompt_kernel(q_ref, k_ref, v_ref, cos_ref, sin_ref, o_ref, ko_ref, vo_ref,
                        kbf_scr, vt_scr, kmean_scr, bias_scr):
    qi = pl.program_id(2)
    nb = kbf_scr.shape[0]
    blk = MOBA_BLOCK

    @pl.when(qi == 0)
    def _():
        def prep(j, carry):
            rows = pl.ds(pl.multiple_of(j * blk, blk), blk)
            kr = _rope_pair(k_ref[rows, :], cos_ref[rows, :], sin_ref[rows, :])
            ko_ref[rows, :] = kr
            kbf_scr[j] = kr.astype(BF16)
            kmean_scr[pl.ds(j, 1), :] = jnp.mean(kr, axis=0, keepdims=True)
            v = v_ref[rows, :]
            vo_ref[rows, :] = v
            vt_scr[j] = v.T.astype(BF16)
            return carry

        lax.fori_loop(0, nb, prep, 0)

    rows_q = pl.ds(pl.multiple_of(qi * blk, blk), blk)
    q = _rope_pair(q_ref[...], cos_ref[rows_q, :], sin_ref[rows_q, :])
    lane = lax.broadcasted_iota(jnp.int32, q.shape, 1)
    kmean = kmean_scr[...]
    kidx = lax.broadcasted_iota(jnp.int32, (blk, blk), 0)
    qidx = lax.broadcasted_iota(jnp.int32, (blk, blk), 1)
    heads = range(2)
    hrows = [slice(h * HEAD_DIM, (h + 1) * HEAD_DIM) for h in heads]
    qh = [jnp.where(lane // HEAD_DIM == h, q, 0.0) for h in heads]
    gate = [lax.dot_general(kmean, qh[h], (((1,), (1,)), ((), ())), precision=lax.Precision.HIGHEST,
                            preferred_element_type=F32) for h in heads]
    for h in heads:
        bias_scr[h] = _topk_bias(gate[h], qi)
    qs = [(qh[h] * HEAD_DIM ** -0.5).astype(BF16) for h in heads]

    k_own = kbf_scr[qi]
    s = [jnp.where(kidx <= qidx, _dot_nt(k_own, qs[h]), NEG_INF) for h in heads]
    m = [jnp.max(s[h], axis=0, keepdims=True) for h in heads]
    p = [jnp.exp(s[h] - m[h]) for h in heads]
    l = [jnp.sum(p[h], axis=0, keepdims=True) for h in heads]
    acc = [_dot(vt_scr[qi, hrows[h], :], p[h].astype(BF16)) for h in heads]

    def step(t, carry):
        m, l, acc = carry
        js = (2 * t, 2 * t + 1)
        hb = [(h, b) for h in heads for b in range(2)]
        ks = [kbf_scr[j] for j in js]
        s = [_dot_nt(ks[b], qs[h]) + bias_scr[h, pl.ds(js[b], 1), :] for h, b in hb]
        bm = [jnp.max(x, axis=0, keepdims=True) for x in s]
        m_new = [jnp.maximum(m[h], jnp.maximum(bm[2 * h], bm[2 * h + 1])) for h in heads]
        alpha = [jnp.exp(m[h] - m_new[h]) for h in heads]
        p = [jnp.exp(x - m_new[h]) for x, (h, b) in zip(s, hb)]
        ps = [jnp.sum(x, axis=0, keepdims=True) for x in p]
        pv = [_dot(vt_scr[js[b], hrows[h], :], x.astype(BF16)) for x, (h, b) in zip(p, hb)]
        l = [alpha[h] * l[h] + (ps[2 * h] + ps[2 * h + 1]) for h in heads]
        acc = [alpha[h] * acc[h] + (pv[2 * h] + pv[2 * h + 1]) for h in heads]
        return m_new, l, acc

    m, l, acc = lax.fori_loop(0, (qi + 1) // 2, step, (m, l, acc))
    o_ref[...] = jnp.concatenate([acc[h] / l[h] for h in heads], axis=0).T.astype(o_ref.dtype)


def _moba_prompt(qkv, cos_t, sin_t, n_seq, seq_len):
    proj3 = qkv.reshape(n_seq, seq_len, 3 * D_MODEL)
    nb = seq_len // MOBA_BLOCK
    qb, kb, vb = 0, N_PAIRS, 2 * N_PAIRS
    seq_spec = lambda col0: pl.BlockSpec((None, seq_len, PAIR_W), lambda b, hp, qi: (b, 0, col0 + hp))
    tab_spec = pl.BlockSpec((seq_len, PAIR_W), lambda b, hp, qi: (0, 0))
    return pl.pallas_call(
        _moba_prompt_kernel,
        out_shape=(
            jax.ShapeDtypeStruct((n_seq, seq_len, D_MODEL), BF16),
            jax.ShapeDtypeStruct((n_seq, seq_len, D_MODEL), F32),
            jax.ShapeDtypeStruct((n_seq, seq_len, D_MODEL), F32),
        ),
        grid=(n_seq, N_PAIRS, nb),
        in_specs=[
            pl.BlockSpec((None, MOBA_BLOCK, PAIR_W), lambda b, hp, qi: (b, qi, qb + hp)),
            seq_spec(kb),
            seq_spec(vb),
            tab_spec,
            tab_spec,
        ],
        out_specs=(
            pl.BlockSpec((None, MOBA_BLOCK, PAIR_W), lambda b, hp, qi: (b, qi, hp)),
            seq_spec(0),
            seq_spec(0),
        ),
        scratch_shapes=[
            pltpu.VMEM((nb, MOBA_BLOCK, PAIR_W), BF16),
            pltpu.VMEM((nb, PAIR_W, MOBA_BLOCK), BF16),
            pltpu.VMEM((nb, PAIR_W), F32),
            pltpu.VMEM((2, nb, MOBA_BLOCK), F32),
        ],
        compiler_params=_cparams("parallel", "parallel", "arbitrary"),
        name="moba_prompt",
    )(proj3, proj3, proj3, cos_t, sin_t)


def _hi_lo(x):
    hi = x.astype(BF16)
    return hi, (x - hi.astype(F32)).astype(BF16)


def _mm3(dot_fn, a, b):
    ah, al = _hi_lo(a)
    bh, bl = _hi_lo(b)
    return dot_fn(ah, bh) + dot_fn(ah, bl) + dot_fn(al, bh)


def _mm1(dot_fn, a, b):
    return dot_fn(a.astype(BF16), b.astype(BF16))


_INV_MM = _mm1


def _head_ones3():
    r = lax.broadcasted_iota(jnp.int32, (3 * PAIR_W, PAIR_W), 0)
    c = lax.broadcasted_iota(jnp.int32, (3 * PAIR_W, PAIR_W), 1)
    return jnp.where((r % PAIR_W) // HEAD_DIM == c // HEAD_DIM, 1.0, 0.0).astype(BF16)


def _head_sums(x, ones3):
    hi = x.astype(BF16)
    r1 = x - hi.astype(F32)
    mid = r1.astype(BF16)
    lo = (r1 - mid.astype(F32)).astype(BF16)
    return _dot(jnp.concatenate([hi, mid, lo], axis=1), ones3)


LORA_W = DECAY_LORA + AAA_LORA


def _split3(x):
    hi = x.astype(BF16)
    r1 = x - hi.astype(F32)
    mid = r1.astype(BF16)
    return hi, mid, (r1 - mid.astype(F32)).astype(BF16)


def _chunk_decay_sums(lw, chunk, n_dec):
    tm = lw.shape[0]
    ci = lax.broadcasted_iota(jnp.int32, (tm, tm), 0)
    cj = lax.broadcasted_iota(jnp.int32, (tm, tm), 1)
    same = ci // chunk == cj // chunk
    di = lax.broadcasted_iota(jnp.int32, (n_dec, tm), 0)
    dj = lax.broadcasted_iota(jnp.int32, (n_dec, tm), 1)
    sel = jnp.concatenate([
        jnp.where(same & (ci >= cj), 1.0, 0.0),
        jnp.where(same, 1.0, 0.0),
        jnp.where(di == dj // chunk, 1.0, 0.0)], axis=0).astype(BF16)
    big = _dot(sel, jnp.concatenate(_split3(lw), axis=1))
    big = big[:, :D_MODEL] + big[:, D_MODEL:2 * D_MODEL] + big[:, 2 * D_MODEL:]
    return big[:tm], big[tm:2 * tm], big[2 * tm:]


def _rwkv_prep_kernel(p_ref, ovr_ref, mu_ref, w0_ref, w2_ref, a0_ref, a2_ref, g2_ref, kk_ref, ka_ref, rk_ref,
                      kap_o, r_o, kt_o, bt_o, kh_o, bh_o, v_o, dec_o, g_o, bonus_o, *, period, chunk):
    p = p_ref[...]
    tm = p.shape[0]
    row = lax.broadcasted_iota(jnp.int32, (tm, 1), 0)
    prev = jnp.where(row % period == 0, ovr_ref[...], pltpu.roll(p, 1, 0))
    xm = p + (prev - p) * mu_ref[...]
    r = xm[:, :D_MODEL]
    k = xm[:, D_MODEL:2 * D_MODEL]
    v = xm[:, 2 * D_MODEL:3 * D_MODEL]
    lora = xm[:, 3 * D_MODEL:3 * D_MODEL + LORA_W]
    xg = xm[:, 3 * D_MODEL + LORA_W:]
    z = w0_ref[...] + jnp.dot(jnp.tanh(lora), w2_ref[...], precision=lax.Precision.HIGHEST,
                              preferred_element_type=F32)
    w_log = -jax.nn.softplus(-z) - 0.5
    a = jax.nn.sigmoid(a0_ref[...] + _dot(lora.astype(BF16), a2_ref[...]))
    g_o[...] = _dot(jax.nn.sigmoid(xg).astype(BF16), g2_ref[...])
    kk = k * kk_ref[...]
    k2 = k * (1.0 + (a - 1.0) * ka_ref[...])
    rk = r * k2 * rk_ref[...]
    lw = -jnp.exp(w_log)
    cum, tot, dec = _chunk_decay_sums(lw, chunk, dec_o.shape[0])
    dec_o[...] = jnp.exp(dec)
    g_in = jnp.exp(cum)
    g_ex = jnp.exp(cum - lw)
    g_inv = jnp.exp(-cum)
    g_end = jnp.exp(tot - cum)
    ones3 = _head_ones3()
    for pr in range(N_PAIRS):
        sl = slice(pr * PAIR_W, (pr + 1) * PAIR_W)
        kks = kk[:, sl]
        kap = kks / jnp.maximum(jnp.sqrt(_head_sums(kks * kks, ones3)), 1e-12)
        b = kap * a[:, sl]
        kap_o[pr] = (kap * g_ex[:, sl]).astype(BF16)
        r_o[pr] = r[:, sl] * g_in[:, sl]
        kt_o[pr] = (k2[:, sl] * g_inv[:, sl]).astype(BF16)
        bt_o[pr] = (b * g_inv[:, sl]).astype(BF16)
        kh_o[pr] = (k2[:, sl] * g_end[:, sl]).astype(BF16)
        bh_o[pr] = (b * g_end[:, sl]).astype(BF16)
        v_o[pr] = v[:, sl].astype(BF16)
        bonus_o[:, sl] = _head_sums(rk[:, sl], ones3) * v[:, sl]


def _rwkv_prep(p_rwkv, ovr, rw, *, tm, period, chunk):
    rows = p_rwkv.shape[0]
    n_dec = max(8, tm // chunk)
    if ovr.shape[1] == 1:
        ovr_spec = pl.BlockSpec((None, 1, SHIFT_W), lambda i: (i, 0, 0))
    else:
        ovr_spec = pl.BlockSpec((None, tm, SHIFT_W), lambda i: (0, i, 0))
    full = lambda a: pl.BlockSpec(a.shape, lambda i: (0,) * a.ndim)
    pm = lambda dt: jax.ShapeDtypeStruct((N_PAIRS, rows, PAIR_W), dt)
    rm = jax.ShapeDtypeStruct((rows, D_MODEL), F32)
    pm_spec = pl.BlockSpec((N_PAIRS, tm, PAIR_W), lambda i: (0, i, 0))
    rm_spec = pl.BlockSpec((tm, D_MODEL), lambda i: (i, 0))
    consts = (rw["mu"], rw["w0"], rw["w2"], rw["a0"], rw["a2"], rw["g2"], rw["k_k"], rw["k_a"], rw["r_k"])
    outs = pl.pallas_call(
        functools.partial(_rwkv_prep_kernel, period=period, chunk=chunk),
        out_shape=(pm(BF16), pm(F32)) + (pm(BF16),) * 5
        + (jax.ShapeDtypeStruct((rows // tm, n_dec, D_MODEL), F32), rm, rm),
        grid=(rows // tm,),
        in_specs=[pl.BlockSpec((tm, SHIFT_W), lambda i: (i, 0)), ovr_spec] + [full(a) for a in consts],
        out_specs=(pm_spec,) * 7 + (pl.BlockSpec((None, n_dec, D_MODEL), lambda i: (i, 0, 0)), rm_spec, rm_spec),
        compiler_params=_cparams("parallel"),
        name="rwkv_prep",
    )(p_rwkv, ovr, *consts)
    dec = outs[7][:, :tm // chunk].reshape(rows // chunk, 1, D_MODEL)
    return outs[:7], dec, outs[8], outs[9]


def _rwkv_weights(mu_shift, w0, w2_decay, a0, a2, g2, k_k, k_a, r_k):
    zeros = jnp.zeros((DECAY_LORA, D_MODEL), F32)
    return dict(
        mu=mu_shift.reshape(1, SHIFT_W), w0=w0.reshape(1, D_MODEL), a0=a0.reshape(1, D_MODEL),
        w2=jnp.concatenate([w2_decay, zeros], axis=0),
        a2=jnp.concatenate([zeros, a2], axis=0).astype(BF16),
        g2=g2.astype(BF16), k_k=k_k.reshape(1, D_MODEL), k_a=k_a.reshape(1, D_MODEL),
        r_k=r_k.reshape(1, D_MODEL))


INV_BASE = 8


def _unit_lower_inverses(l_ms, ti, tj):
    cl = l_ms[0].shape[0]
    base = min(INV_BASE, cl)
    eye = jnp.where(ti == tj, 1.0, 0.0)
    diag_blk = ti // base == tj // base
    l_bs = [jnp.where(diag_blk, l_m, 0.0) for l_m in l_ms]
    t_ms = [eye - l_b for l_b in l_bs]
    for _ in range(base.bit_length() - 2):
        l_bs = [_INV_MM(_dot, l_b, l_b) for l_b in l_bs]
        t_ms = [t_m + _INV_MM(_dot, t_m, l_b) for t_m, l_b in zip(t_ms, l_bs)]
    size = base
    while size < cl:
        lower_left = (ti // (2 * size) == tj // (2 * size)) & (ti // size != tj // size)
        tmp = [_INV_MM(_dot, t_m, jnp.where(lower_left, l_m, 0.0)) for t_m, l_m in zip(t_ms, l_ms)]
        t_ms = [t_m - _INV_MM(_dot, x, t_m) for t_m, x in zip(t_ms, tmp)]
        size *= 2
    return t_ms


def _rwkv_chunk_kernel(kap_ref, r_ref, kt_ref, bt_ref, kh_ref, bh_ref, v_ref, dec_ref, s0_ref, o_ref, so_ref,
                       st_scr):
    c = pl.program_id(1)
    n_c = pl.num_programs(1)
    cl = r_ref.shape[1]
    hd = HEAD_DIM

    @pl.when(c == 0)
    def _():
        z = jnp.zeros((hd, hd), F32)
        for pr in range(N_PAIRS):
            top = jnp.concatenate([s0_ref[2 * pr], z], axis=1)
            bot = jnp.concatenate([z, s0_ref[2 * pr + 1]], axis=1)
            st_scr[pr] = jnp.concatenate([top, bot], axis=0)

    ti = lax.broadcasted_iota(jnp.int32, (cl, cl), 0)
    tj = lax.broadcasted_iota(jnp.int32, (cl, cl), 1)
    tril = ti >= tj
    stril = ti > tj
    head1 = lax.broadcasted_iota(jnp.int32, (cl, PAIR_W), 1) >= hd
    sr = lax.broadcasted_iota(jnp.int32, (PAIR_W, PAIR_W), 0)
    sc = lax.broadcasted_iota(jnp.int32, (PAIR_W, PAIR_W), 1)
    same_head = (sr // hd) == (sc // hd)

    pairs = range(N_PAIRS)
    hs = [(pr, h) for pr in pairs for h in range(2)]
    bf = lambda xs: [x.astype(BF16) for x in xs]
    r_t = [r_ref[pr] for pr in pairs]
    kap_b = [kap_ref[pr] for pr in pairs]
    kt_b = [kt_ref[pr] for pr in pairs]
    bt_b = [bt_ref[pr] for pr in pairs]
    v_b = [v_ref[pr] for pr in pairs]
    masks = (jnp.logical_not(head1), head1)
    r_h = [jnp.where(masks[h], r_t[pr], 0.0) for pr, h in hs]
    r_hb = bf(r_h)
    kap_hb = [jnp.where(masks[h], kap_b[pr], jnp.zeros_like(kap_b[pr])) for pr, h in hs]
    l_m = [jnp.where(stril, _dot_nt(kap_hb[i], bt_b[pr]), 0.0) for i, (pr, h) in enumerate(hs)]
    a_kk = [jnp.where(stril, _dot_nt(kap_hb[i], kt_b[pr]), 0.0) for i, (pr, h) in enumerate(hs)]
    a_rk = [jnp.where(tril, _dot_nt(r_hb[i], kt_b[pr]), 0.0) for i, (pr, h) in enumerate(hs)]
    a_rb = [jnp.where(tril, _dot_nt(r_hb[i], bt_b[pr]), 0.0) for i, (pr, h) in enumerate(hs)]
    av = [_dot(a, v_b[pr]) for a, (pr, h) in zip(bf(a_kk), hs)]
    oi_h = [_dot(a, v_b[pr]) for a, (pr, h) in zip(bf(a_rk), hs)]
    t_b = bf(_unit_lower_inverses(l_m, ti, tj))
    a_rb_b = bf(a_rb)
    w_h = [_dot(t, k) for t, k in zip(t_b, kap_hb)]
    u_h = [_dot(t, x) for t, x in zip(t_b, bf(av))]
    rp_h = [r - _dot(a, w) for r, a, w in zip(r_h, a_rb_b, bf(w_h))]
    oi_h = [o - _dot(a, u) for o, a, u in zip(oi_h, a_rb_b, bf(u_h))]
    w_m = bf([w_h[2 * pr] + w_h[2 * pr + 1] for pr in pairs])
    u_m = bf([jnp.where(head1, u_h[2 * pr + 1], u_h[2 * pr]) for pr in pairs])
    rp = [rp_h[2 * pr] + rp_h[2 * pr + 1] for pr in pairs]
    oi = [jnp.where(head1, oi_h[2 * pr + 1], oi_h[2 * pr]) for pr in pairs]
    bh_b = [bh_ref[pr] for pr in pairs]
    kh_b = [kh_ref[pr] for pr in pairs]
    wtb = [jnp.where(same_head, _dot_tn(w_m[pr], bh_b[pr]), 0.0) for pr in pairs]
    n_t = [jnp.where(same_head, _dot_tn(v_b[pr], kh_b[pr]) - _dot_tn(u_m[pr], bh_b[pr]), 0.0) for pr in pairs]
    s = [st_scr[pr] for pr in pairs]
    o_new = [_mm3(_dot_nt, rp[pr], s[pr]) + oi[pr] for pr in pairs]
    s_new = [s[pr] * dec_ref[:, pr * PAIR_W:(pr + 1) * PAIR_W] - _mm3(_dot, s[pr], wtb[pr]) + n_t[pr]
             for pr in pairs]
    for pr in pairs:
        o_ref[pr] = o_new[pr]
        st_scr[pr] = s_new[pr]

    @pl.when(c == n_c - 1)
    def _():
        for pr in range(N_PAIRS):
            s = st_scr[pr]
            so_ref[2 * pr] = s[:hd, :hd]
            so_ref[2 * pr + 1] = s[hd:, hd:]


def _rwkv_chunks(pm_inputs, dec, s0, *, n_seq, chunk):
    rows = pm_inputs[0].shape[1]
    n_c = rows // n_seq // chunk
    pm_spec = pl.BlockSpec((N_PAIRS, chunk, PAIR_W), lambda s, c: (0, s * n_c + c, 0))
    st_spec = pl.BlockSpec((None, N_HEADS, HEAD_DIM, HEAD_DIM), lambda s, c: (s, 0, 0, 0))
    return pl.pallas_call(
        _rwkv_chunk_kernel,
        out_shape=(jax.ShapeDtypeStruct((N_PAIRS, rows, PAIR_W), F32),
                   jax.ShapeDtypeStruct((n_seq, N_HEADS, HEAD_DIM, HEAD_DIM), F32)),
        grid=(n_seq, n_c),
        in_specs=[pm_spec] * 7 + [pl.BlockSpec((None, 1, D_MODEL), lambda s, c: (s * n_c + c, 0, 0)), st_spec],
        out_specs=(pm_spec, st_spec),
        scratch_shapes=[pltpu.VMEM((N_PAIRS, PAIR_W, PAIR_W), F32)],
        compiler_params=_cparams("parallel", "arbitrary"),
        name="rwkv_chunks",
    )(*pm_inputs, dec, s0)


def _mixer_out_kernel(x_ref, gt_ref, npost_ref, o_ref, bonus_ref, g_ref, attn_ref, ga_ref, gb_ref,
                      lnw_ref, lnb_ref, wor_ref, wom_ref, wout_ref, y_ref):
    ones3 = _head_ones3()
    parts = []
    for pr in range(N_PAIRS):
        o = o_ref[pr]
        mu = _head_sums(o, ones3) * (1.0 / HEAD_DIM)
        d = o - mu
        var = _head_sums(d * d, ones3) * (1.0 / HEAD_DIM)
        parts.append(d * lax.rsqrt(var + LNX_EPS))
    on = jnp.concatenate(parts, axis=1) * lnw_ref[...] + lnb_ref[...]
    ya = _dot(((on + bonus_ref[...]) * g_ref[...]).astype(BF16), wor_ref[...])
    yb = _dot(attn_ref[...], wom_ref[...])
    y = jax.nn.sigmoid(ga_ref[...]) * ya + jax.nn.sigmoid(gb_ref[...]) * yb
    y = _dot(y.astype(BF16), wout_ref[...])
    y_ref[...] = x_ref[...] + gt_ref[...] * _rms(y, npost_ref[...])


def _mixer_out(x, mod, which, n_post, o_pm, bonus, g, attn, gates, lnx_w, lnx_b, w_o_rwkv, w_o_moba, w_out,
               *, tm, tiles_per_group):
    rows = x.shape[0]
    row_spec = pl.BlockSpec((tm, D_MODEL), lambda i: (i, 0))
    w_spec = pl.BlockSpec((D_MODEL, D_MODEL), lambda i: (0, 0))
    return pl.pallas_call(
        _mixer_out_kernel,
        out_shape=jax.ShapeDtypeStruct((rows, D_MODEL), F32),
        grid=(rows // tm,),
        in_specs=[
            row_spec,
            _mod_spec(mod, which, tm, tiles_per_group),
            _vec_spec(),
            pl.BlockSpec((N_PAIRS, tm, PAIR_W), lambda i: (0, i, 0)),
            row_spec, row_spec, row_spec,
            pl.BlockSpec((tm, D_MODEL), lambda i: (i, 0)),
            pl.BlockSpec((tm, D_MODEL), lambda i: (i, 1)),
            _vec_spec(), _vec_spec(), w_spec, w_spec, w_spec,
        ],
        out_specs=row_spec,
        compiler_params=_cparams("parallel"),
        name="mixer_out",
    )(x, mod, n_post, o_pm, bonus, g, attn, gates, gates, lnx_w, lnx_b, w_o_rwkv, w_o_moba, w_out)


def _rope_rows_kernel(qkv_ref, cos_ref, sin_ref, q_ref, k_ref):
    cos = cos_ref[...]
    sin = sin_ref[...]
    for pr in range(N_PAIRS):
        sl = slice(pr * PAIR_W, (pr + 1) * PAIR_W)
        q_ref[:, sl] = _rope_pair(qkv_ref[:, sl], cos, sin)
        k_ref[:, sl] = _rope_pair(qkv_ref[:, D_MODEL + pr * PAIR_W:D_MODEL + (pr + 1) * PAIR_W], cos, sin)


def _rope_rows(qkv, cos_t, sin_t):
    rows = qkv.shape[0]
    out = jax.ShapeDtypeStruct((rows, D_MODEL), F32)
    return pl.pallas_call(_rope_rows_kernel, out_shape=(out, out), name="rope_rows")(qkv, cos_t, sin_t)


TOK_SLOTS = 8


PAGES_PER_STEP = 4
PAGES_PER_BLOCK = MOBA_BLOCK // PAGE_SIZE


def _moba_sample_kernel(pt_ref, q_ref, kn_ref, vn_ref, *refs, n_seq, n_pages, n_tok):
    pps = PAGES_PER_STEP
    kp_refs, vp_refs = refs[:pps], refs[pps:2 * pps]
    o_ref, q2_scr, s_scr, pn_scr, acc_scr, l_scr = refs[2 * pps:]
    b = pl.program_id(0)
    n = pl.program_id(1)
    n_steps = pl.num_programs(1)
    rows = N_HEADS * TOK_SLOTS
    n_blk = n_pages // PAGES_PER_BLOCK
    blk_per_step = pps // PAGES_PER_BLOCK
    scale = HEAD_DIM ** -0.5
    row_head = lax.broadcasted_iota(jnp.int32, (rows, D_MODEL), 0) // TOK_SLOTS
    lane_head = lax.broadcasted_iota(jnp.int32, (rows, D_MODEL), 1) // HEAD_DIM
    slot_k = b % 2
    slot_v = (b + 1) % 2

    @pl.when((n == 0) & (b < n_seq))
    def _():
        q_bd = jnp.where(row_head == lane_head, jnp.tile(q_ref[...], (N_HEADS, 1)), 0.0)
        hi, lo = _hi_lo(q_bd)
        q2_scr[:rows] = hi
        q2_scr[rows:] = lo

    @pl.when(b < n_seq)
    def _():
        kh, kl = _hi_lo(jnp.concatenate([r[...] for r in kp_refs], axis=1))
        top = _dot(q2_scr[...], kh)
        s = top[:rows] + top[rows:] + _dot(q2_scr[:rows], kl)
        for i in range(blk_per_step):
            s_scr[slot_k, n * blk_per_step + i] = s[:, i * MOBA_BLOCK:(i + 1) * MOBA_BLOCK]

    @pl.when((n == n_steps - 1) & (b < n_seq))
    def _():
        cols = [jnp.sum(s_scr[slot_k, i], axis=1, keepdims=True) for i in range(n_blk)]
        gate = jnp.concatenate(cols, axis=1)
        col = lax.broadcasted_iota(jnp.int32, gate.shape, 1)
        cnt = jnp.zeros(gate.shape, jnp.int32)
        for n in range(n_blk):
            gn = gate[:, n:n + 1]
            cnt = cnt + jnp.where((gn > gate) | ((gn == gate) & (n < col)), 1, 0)
        bias = jnp.where(cnt < MOBA_TOPK, 0.0, NEG_INF)
        knh, knl = _hi_lo(kn_ref[...])
        qh = q2_scr[:rows]
        s_new = (_dot_nt(qh, knh) + _dot_nt(q2_scr[rows:], knh) + _dot_nt(qh, knl)) * scale
        t_q = lax.broadcasted_iota(jnp.int32, s_new.shape, 0) % TOK_SLOTS
        t_k = lax.broadcasted_iota(jnp.int32, s_new.shape, 1)
        s_new = jnp.where((t_k <= t_q) & (t_k < n_tok), s_new, NEG_INF)
        logits = [s_scr[slot_k, i] * scale + bias[:, i:i + 1] for i in range(n_blk)]
        mx = functools.reduce(jnp.maximum, logits)
        m = jnp.maximum(jnp.max(mx, axis=1, keepdims=True), jnp.max(s_new, axis=1, keepdims=True))
        p_new = jnp.exp(s_new - m)
        ps = [jnp.exp(x - m) for x in logits]
        for i in range(n_blk):
            s_scr[slot_k, i] = ps[i]
        tot = functools.reduce(jnp.add, ps)
        l_scr[slot_k] = jnp.sum(tot, axis=1, keepdims=True) + jnp.sum(p_new, axis=1, keepdims=True)
        pn_scr[slot_k] = p_new

    @pl.when(b >= 1)
    def _():
        @pl.when(n == 0)
        def _():
            acc_scr[...] = _dot(pn_scr[slot_v], vn_ref[...])

        for i in range(blk_per_step):
            vb = jnp.concatenate([r[...] for r in vp_refs[i * PAGES_PER_BLOCK:(i + 1) * PAGES_PER_BLOCK]], axis=1)
            acc_scr[...] += _dot_nt(s_scr[slot_v, n * blk_per_step + i].astype(BF16), vb.astype(BF16))

        @pl.when(n == n_steps - 1)
        def _():
            acc = jnp.where(row_head == lane_head, acc_scr[...] / l_scr[slot_v], 0.0)
            out = acc[:TOK_SLOTS]
            for h in range(1, N_HEADS):
                out = out + acc[h * TOK_SLOTS:(h + 1) * TOK_SLOTS]
            o_ref[...] = out


def _moba_sample(page_table, q_rot, k_new, v_new, cache_kt, cache_vt, *, n_tok):
    n_seq, n_pages = page_table.shape
    pps = PAGES_PER_STEP
    assert n_pages % pps == 0 and pps % PAGES_PER_BLOCK == 0 and n_tok <= TOK_SLOTS
    rows = N_HEADS * TOK_SLOTS
    n_blk = n_pages // PAGES_PER_BLOCK
    cur = lambda b: jnp.minimum(b, n_seq - 1)
    prev = lambda b: jnp.maximum(b - 1, 0)
    cur_spec = pl.BlockSpec((None, TOK_SLOTS, D_MODEL), lambda b, n, pt: (cur(b), 0, 0))
    prev_spec = pl.BlockSpec((None, TOK_SLOTS, D_MODEL), lambda b, n, pt: (prev(b), 0, 0))

    def page_spec(seq_of, i):
        return pl.BlockSpec((None, D_MODEL, PAGE_SIZE),
                            lambda b, n, pt: (pt[seq_of(b) * n_pages + n * pps + i], 0, 0))

    return pl.pallas_call(
        functools.partial(_moba_sample_kernel, n_seq=n_seq, n_pages=n_pages, n_tok=n_tok),
        out_shape=jax.ShapeDtypeStruct((n_seq, TOK_SLOTS, D_MODEL), F32),
        grid_spec=pltpu.PrefetchScalarGridSpec(
            num_scalar_prefetch=1,
            grid=(n_seq + 1, n_pages // pps),
            in_specs=[cur_spec, cur_spec, prev_spec]
            + [page_spec(cur, i) for i in range(pps)] + [page_spec(prev, i) for i in range(pps)],
            out_specs=prev_spec,
            scratch_shapes=[
                pltpu.VMEM((2 * rows, D_MODEL), BF16),
                pltpu.VMEM((2, n_blk, rows, MOBA_BLOCK), F32),
                pltpu.VMEM((2, rows, TOK_SLOTS), F32),
                pltpu.VMEM((rows, D_MODEL), F32),
                pltpu.VMEM((2, rows, 1), F32),
            ],
        ),
        compiler_params=_cparams("arbitrary", "arbitrary"),
        name="moba_sample",
    )(page_table.reshape(-1), q_rot, k_new, v_new, *([cache_kt] * pps), *([cache_vt] * pps))


DENSE_TM = 512
RWKV_TM = 256
PROMPT_CHUNK = 64
SAMPLE_CHUNK = 16


def _layer(x, mod, lw, *, tiles, shift_ovr, shift_period, prep_chunk, attend, rwkv_scan):
    tpg_dense, tpg_rwkv = tiles
    x1 = _ffn(x, mod, 0, lw["n1_pre"], lw["n1_post"], lw["w1_gu"], lw["w1_down"], tm=DENSE_TM,
              tiles_per_group=tpg_dense)
    p_rwkv, qkv, gates = _inproj(x1, mod, 3, lw["n2_pre"], lw["w_in"], tm=DENSE_TM, tiles_per_group=tpg_dense)
    attn, extras = attend(qkv)
    pm, dec, g, bonus = _rwkv_prep(p_rwkv, shift_ovr(p_rwkv), lw["rwkv"], tm=RWKV_TM, period=shift_period,
                                   chunk=prep_chunk)
    o_pm, state = rwkv_scan(pm, dec)
    x2 = _mixer_out(x1, mod, 5, lw["n2_post"], o_pm, bonus, g, attn, gates, lw["lnx_w"], lw["lnx_b"],
                    lw["w_o_rwkv"], lw["w_o_moba"], lw["w_out"], tm=RWKV_TM, tiles_per_group=tpg_rwkv)
    y = _ffn(x2, mod, 6, lw["n3_pre"], lw["n3_post"], lw["w3_gu"], lw["w3_down"], tm=DENSE_TM,
             tiles_per_group=tpg_dense)
    return y, p_rwkv, qkv, state, extras


def kernel(x_prompt, x_sample, cache_k, cache_v, state_rwkv, state_shift, page_table, c_prompt, c_sample, w_ada, b_ada, n1_pre, n1_post, w1_gu, w1_down, n2_pre, n2_post, w_in, mu_shift, w0, w2_decay, a0, a2, g2, k_k, k_a, r_k, lnx_w, lnx_b, w_o_rwkv, w_o_moba, w_out, n3_pre, n3_post, w3_gu, w3_down):
    assert w_ada.shape[0] == 1, "single layer"
    bp, seq, _ = x_prompt.shape
    bs, ts, _ = x_sample.shape
    n_pool = cache_k.shape[1]
    n_pages = page_table.shape[1]
    rows_p, rows_s = bp * seq, bs * ts
    assert seq % DENSE_TM == 0 and seq % MOBA_BLOCK == 0 and rows_s % DENSE_TM == 0 and ts <= SAMPLE_CHUNK

    lw = dict(
        n1_pre=n1_pre, n1_post=n1_post, n2_pre=n2_pre, n2_post=n2_post, n3_pre=n3_pre, n3_post=n3_post,
        w1_gu=w1_gu[0].astype(BF16), w1_down=w1_down[0].astype(BF16),
        w3_gu=w3_gu[0].astype(BF16), w3_down=w3_down[0].astype(BF16),
        w_in=_split_w_in(w_in[0]),
        rwkv=_rwkv_weights(mu_shift[0], w0[0], w2_decay[0], a0[0], a2[0], g2[0], k_k[0], k_a[0], r_k[0]),
        lnx_w=lnx_w, lnx_b=lnx_b,
        w_o_rwkv=w_o_rwkv[0].astype(BF16), w_o_moba=w_o_moba[0].astype(BF16), w_out=w_out[0].astype(BF16),
    )

    n_c = bp + bs
    c_all = jnp.concatenate([c_prompt, c_sample, jnp.zeros((-n_c % 8, D_MODEL), F32)], axis=0)
    mod = _ada(c_all, w_ada[0], b_ada[0])
    mod_p = mod[:bp].reshape(bp, N_ADA, 1, D_MODEL).transpose(1, 0, 2, 3)
    mod_s = jnp.repeat(mod[bp:n_c].reshape(bs, N_ADA, D_MODEL), ts, axis=0).transpose(1, 0, 2)[:, None]

    cos_p, sin_p = _rope_tables(jnp.arange(seq, dtype=jnp.int32))

    def attend_p(qkv):
        attn, k_rot, v = _moba_prompt(qkv, cos_p, sin_p, bp, seq)
        return attn.reshape(rows_p, D_MODEL), (k_rot, v)

    def shift_ovr_p(p_rwkv):
        tiles_per_seq = seq // RWKV_TM
        prev = jnp.concatenate([jnp.zeros((1, SHIFT_W), F32), p_rwkv[RWKV_TM - 1::RWKV_TM][:-1]], axis=0)
        first = (jnp.arange(rows_p // RWKV_TM) % tiles_per_seq == 0)[:, None]
        return jnp.where(first, 0.0, prev)[:, None, :]

    def scan_p(pm, dec):
        s0 = jnp.zeros((bp, N_HEADS, HEAD_DIM, HEAD_DIM), F32)
        return _rwkv_chunks(pm, dec, s0, n_seq=bp, chunk=PROMPT_CHUNK)

    y_p, p_p, _, state_p, (k_p, v_p) = _layer(
        x_prompt.reshape(rows_p, D_MODEL), mod_p, lw, tiles=(seq // DENSE_TM, seq // RWKV_TM),
        shift_ovr=shift_ovr_p, shift_period=RWKV_TM, prep_chunk=PROMPT_CHUNK, attend=attend_p, rwkv_scan=scan_p)

    pos_s = n_pages * PAGE_SIZE + jnp.arange(ts, dtype=jnp.int32)
    cos_s, sin_s = (jnp.tile(t, (bs, 1)) for t in _rope_tables(pos_s))
    pad_tok = lambda a: jnp.pad(a.reshape(bs, ts, D_MODEL), ((0, 0), (0, TOK_SLOTS - ts), (0, 0)))
    cache_kt = cache_k[0].transpose(0, 2, 3, 1).reshape(n_pool, D_MODEL, PAGE_SIZE)
    cache_vt = cache_v[0].transpose(0, 2, 3, 1).reshape(n_pool, D_MODEL, PAGE_SIZE)

    def attend_s(qkv):
        q_rot, k_rot = _rope_rows(qkv, cos_s, sin_s)
        v = qkv[:, 2 * D_MODEL:]
        attn = _moba_sample(page_table, pad_tok(q_rot), pad_tok(k_rot), pad_tok(v), cache_kt, cache_vt, n_tok=ts)
        return attn[:, :ts].reshape(rows_s, D_MODEL).astype(BF16), (k_rot, v)

    def scan_s(pm, dec):
        pad = lambda a: jnp.pad(a.reshape(N_PAIRS, bs, ts, PAIR_W),
                                ((0, 0), (0, 0), (0, SAMPLE_CHUNK - ts), (0, 0))).reshape(N_PAIRS, -1, PAIR_W)
        o_pm, state = _rwkv_chunks([pad(a) for a in pm], dec, state_rwkv[0], n_seq=bs, chunk=SAMPLE_CHUNK)
        o_pm = o_pm.reshape(N_PAIRS, bs, SAMPLE_CHUNK, PAIR_W)[:, :, :ts].reshape(N_PAIRS, rows_s, PAIR_W)
        return o_pm, state

    y_s, p_s, _, state_s, (k_s, v_s) = _layer(
        x_sample.reshape(rows_s, D_MODEL), mod_s, lw, tiles=(1, 1),
        shift_ovr=lambda p: jnp.repeat(state_shift[0], ts, axis=0)[None], shift_period=ts, prep_chunk=ts,
        attend=attend_s, rwkv_scan=scan_s)

    heads = lambda a, b, t: a.reshape(1, b, t, N_HEADS, HEAD_DIM)
    return (
        y_p.reshape(bp, seq, D_MODEL),
        y_s.reshape(bs, ts, D_MODEL),
        heads(k_p, bp, seq), heads(v_p, bp, seq),
        state_p[None], p_p.reshape(bp, seq, SHIFT_W)[:, -1][None],
        heads(k_s, bs, ts), heads(v_s, bs, ts),
        state_s[None], p_s.reshape(bs, ts, SHIFT_W)[:, -1][None],
    )
```

```python
import functools

import jax
import jax.numpy as jnp
from jax import lax
from jax.experimental import pallas as pl
from jax.experimental.pallas import tpu as pltpu

D_MODEL = 1024
HEAD_DIM = 64
N_HEADS = D_MODEL // HEAD_DIM
PAIR_W = 2 * HEAD_DIM
N_PAIRS = N_HEADS // 2
DECAY_LORA = 64
AAA_LORA = 64
GATE_LORA = 128
SHIFT_W = 3 * D_MODEL + DECAY_LORA + AAA_LORA + GATE_LORA
IN_W = SHIFT_W + 3 * D_MODEL + 2 * D_MODEL
D_FF = 11 * D_MODEL // 4
N_ADA = 9
MACARON_W = 0.5
RMS_EPS = 1e-6
LNX_EPS = 64e-5
MOBA_BLOCK = 256
MOBA_TOPK = 3
PAGE_SIZE = 128
ROPE_THETA = 10000.0

SUBLANES = 8
VMEM_LIMIT = 56 * 1024 * 1024

BF16 = jnp.bfloat16
F32 = jnp.float32
NEG_INF = float("-inf")


def _cparams(*sem):
    return pltpu.CompilerParams(dimension_semantics=sem, vmem_limit_bytes=VMEM_LIMIT)


def _rms(x, g):
    return x * lax.rsqrt(jnp.mean(x * x, axis=-1, keepdims=True) + RMS_EPS) * g


def _dot(a, b):
    return jnp.dot(a, b, preferred_element_type=F32)


def _dot_nt(a, b):
    return lax.dot_general(a, b, (((1,), (1,)), ((), ())), preferred_element_type=F32)


def _dot_tn(a, b):
    return lax.dot_general(a, b, (((0,), (0,)), ((), ())), preferred_element_type=F32)


ADA_TN = 1536


def _ada_kernel(c_ref, w_ref, b_ref, o_ref):
    c = c_ref[...]
    a = (c * jax.nn.sigmoid(c)).astype(BF16)
    o_ref[...] = _dot(a, w_ref[...].astype(BF16)) + b_ref[...]


def _ada(c_all, w_ada, b_ada):
    rows = c_all.shape[0]
    n_out = w_ada.shape[1]
    return pl.pallas_call(
        _ada_kernel,
        out_shape=jax.ShapeDtypeStruct((rows, n_out), F32),
        grid=(n_out // ADA_TN,),
        in_specs=[
            pl.BlockSpec((rows, D_MODEL), lambda j: (0, 0)),
            pl.BlockSpec((D_MODEL, ADA_TN), lambda j: (0, j)),
            pl.BlockSpec((1, ADA_TN), lambda j: (0, j)),
        ],
        out_specs=pl.BlockSpec((rows, ADA_TN), lambda j: (0, j)),
        compiler_params=_cparams("arbitrary"),
        name="ada_mod",
    )(c_all, w_ada, b_ada.reshape(1, n_out))


def _mod_spec(mod, which, tm, tiles_per_group):
    rows_in_group = mod.shape[2]
    if rows_in_group == 1:
        return pl.BlockSpec((None, None, 1, D_MODEL), lambda i, *_: (which, i // tiles_per_group, 0, 0))
    return pl.BlockSpec((None, None, tm, D_MODEL), lambda i, *_: (which, 0, i, 0))


def _vec_spec():
    return pl.BlockSpec((1, D_MODEL), lambda i, *_: (0, 0))


FFN_TF = D_FF // 2


def _ffn_kernel(x_ref, sh_ref, sc_ref, gt_ref, npre_ref, npost_ref, wg_ref, wu_ref, wd_ref, o_ref, h_scr, acc_scr):
    j = pl.program_id(1)

    @pl.when(j == 0)
    def _():
        h = _rms(x_ref[...], npre_ref[...]) * (1.0 + sc_ref[...]) + sh_ref[...]
        h_scr[...] = h.astype(BF16)
        acc_scr[...] = jnp.zeros_like(acc_scr)

    h = h_scr[...]
    g = _dot(h, wg_ref[...])
    u = _dot(h, wu_ref[...])
    a = (g * jax.nn.sigmoid(g) * u).astype(BF16)
    acc_scr[...] += _dot(a, wd_ref[...])

    @pl.when(j == pl.num_programs(1) - 1)
    def _():
        y = _rms(acc_scr[...], npost_ref[...])
        o_ref[...] = x_ref[...] + MACARON_W * gt_ref[...] * y


def _ffn(x, mod, which0, n_pre, n_post, wgu, wd, *, tm, tiles_per_group):
    rows = x.shape[0]
    nf = D_FF // FFN_TF
    row_spec = pl.BlockSpec((tm, D_MODEL), lambda i, j: (i, 0))
    return pl.pallas_call(
        _ffn_kernel,
        out_shape=jax.ShapeDtypeStruct((rows, D_MODEL), F32),
        grid=(rows // tm, nf),
        in_specs=[
            row_spec,
            _mod_spec(mod, which0, tm, tiles_per_group),
            _mod_spec(mod, which0 + 1, tm, tiles_per_group),
            _mod_spec(mod, which0 + 2, tm, tiles_per_group),
            _vec_spec(),
            _vec_spec(),
            pl.BlockSpec((D_MODEL, FFN_TF), lambda i, j: (0, j)),
            pl.BlockSpec((D_MODEL, FFN_TF), lambda i, j: (0, j + nf)),
            pl.BlockSpec((FFN_TF, D_MODEL), lambda i, j: (j, 0)),
        ],
        out_specs=row_spec,
        scratch_shapes=[pltpu.VMEM((tm, D_MODEL), BF16), pltpu.VMEM((tm, D_MODEL), F32)],
        compiler_params=_cparams("parallel", "arbitrary"),
        name="ffn",
    )(x, mod, mod, mod, n_pre, n_post, wgu, wgu, wd)


INPROJ_WIDTHS = (SHIFT_W, 3 * D_MODEL, 2 * D_MODEL)
INPROJ_SPLIT = 2


def _inproj_kernel(x_ref, sh_ref, sc_ref, npre_ref, w0_ref, w1_ref, w2_ref, o0_ref, o1_ref, o2_ref, h_scr):
    j = pl.program_id(1)

    @pl.when(j == 0)
    def _():
        h = _rms(x_ref[...], npre_ref[...]) * (1.0 + sc_ref[...]) + sh_ref[...]
        h_scr[...] = h.astype(BF16)

    for sec, (w_ref, o_ref) in enumerate(((w0_ref, o0_ref), (w1_ref, o1_ref), (w2_ref, o2_ref))):
        @pl.when(j // INPROJ_SPLIT == sec)
        def _(w_ref=w_ref, o_ref=o_ref):
            o_ref[...] = _dot(h_scr[...], w_ref[...])


def _inproj(x, mod, which0, n_pre, w_secs, *, tm, tiles_per_group):
    rows = x.shape[0]

    def sec_col(sec):
        return lambda i, j: jnp.clip(j - sec * INPROJ_SPLIT, 0, INPROJ_SPLIT - 1)

    w_specs, o_specs = [], []
    for sec, width in enumerate(INPROJ_WIDTHS):
        tn = width // INPROJ_SPLIT
        col = sec_col(sec)
        w_specs.append(pl.BlockSpec((D_MODEL, tn), lambda i, j, col=col: (0, col(i, j))))
        o_specs.append(pl.BlockSpec((tm, tn), lambda i, j, col=col: (i, col(i, j))))
    return pl.pallas_call(
        _inproj_kernel,
        out_shape=tuple(jax.ShapeDtypeStruct((rows, w), F32) for w in INPROJ_WIDTHS),
        grid=(rows // tm, len(INPROJ_WIDTHS) * INPROJ_SPLIT),
        in_specs=[
            pl.BlockSpec((tm, D_MODEL), lambda i, j: (i, 0)),
            _mod_spec(mod, which0, tm, tiles_per_group),
            _mod_spec(mod, which0 + 1, tm, tiles_per_group),
            _vec_spec(),
            *w_specs,
        ],
        out_specs=tuple(o_specs),
        scratch_shapes=[pltpu.VMEM((tm, D_MODEL), BF16)],
        compiler_params=_cparams("parallel", "arbitrary"),
        name="inproj",
    )(x, mod, mod, n_pre, *w_secs)


def _split_w_in(w_in):
    w = w_in.astype(BF16)
    c1 = SHIFT_W
    c2 = SHIFT_W + 3 * D_MODEL
    return w[:, :c1], w[:, c1:c2], w[:, c2:]


def _rope_tables(pos):
    half = HEAD_DIM // 2
    inv = ROPE_THETA ** (-jnp.arange(half, dtype=F32) / half)
    ang = pos.astype(F32)[:, None] * inv[None, :]
    cos, sin = jnp.cos(ang), jnp.sin(ang)
    return jnp.tile(cos, (1, 4)), jnp.tile(jnp.concatenate([-sin, sin], axis=1), (1, 2))


def _rope_pair(x, cos, sin_signed):
    lane = lax.broadcasted_iota(jnp.int32, x.shape, 1)
    first_half = (lane % HEAD_DIM) < HEAD_DIM // 2
    partner = jnp.where(first_half, pltpu.roll(x, PAIR_W - HEAD_DIM // 2, 1), pltpu.roll(x, HEAD_DIM // 2, 1))
    return x * cos + partner * sin_signed


LOG2_E = 1.4426950408889634
EXP_RANGE_LOG2 = 100.0
BOUND_SLACK = 1.0 + 2.0 ** -6
VT_ROWS = HEAD_DIM + 16


def _topk_bias(gate, n_past):
    nb = gate.shape[0]
    blk = lax.broadcasted_iota(jnp.int32, gate.shape, 0)
    cnt = jnp.zeros(gate.shape, jnp.int32)
    for jp in range(nb):
        gj = gate[jp:jp + 1, :]
        beats = (gj > gate) | ((gj == gate) & (jp < blk))
        cnt = cnt + jnp.where(beats & (jp < n_past), 1, 0)
    sel = (blk < n_past) & (cnt < MOBA_TOPK)
    return jnp.where(sel, 0.0, NEG_INF)


def _moba_prompt_kernel(q_ref, k_ref, v_ref, cos_ref, sin_ref, o_ref, ko_ref, vo_ref,
                        kbf_scr, vt_scr, kmean_scr, bias_scr, knorm_scr, acc_scr, s_buf, p_buf):
    qi = pl.program_id(2)
    nb = kbf_scr.shape[0]
    blk = MOBA_BLOCK

    @pl.when(qi == 0)
    def _():
        def prep(j, carry):
            rows = pl.ds(pl.multiple_of(j * blk, blk), blk)
            kr = _rope_pair(k_ref[rows, :], cos_ref[rows, :], sin_ref[rows, :])
            ko_ref[rows, :] = kr
            kbf_scr[j] = kr.astype(BF16)
            kmean_scr[pl.ds(j, 1), :] = jnp.mean(kr, axis=0, keepdims=True)
            kn2 = jnp.max(_head_sums(kr * kr, ones3), axis=0, keepdims=True)
            knorm_scr[...] = jnp.maximum(knorm_scr[...], kn2)
            v = v_ref[rows, :]
            vo_ref[rows, :] = v
            vt = v.T.astype(BF16)
            for h in range(2):
                vt_scr[j, h, :HEAD_DIM, :] = vt[h * HEAD_DIM:(h + 1) * HEAD_DIM]
                vt_scr[j, h, HEAD_DIM:, :] = ones_rows
            return carry

        pad_row = lax.broadcasted_iota(jnp.int32, (VT_ROWS - HEAD_DIM, blk), 0)
        ones_rows = jnp.where(pad_row == 0, 1.0, 0.0).astype(BF16)
        ones3 = _head_ones3()
        knorm_scr[...] = jnp.zeros_like(knorm_scr)
        lax.fori_loop(0, nb, prep, 0)

    rows_q = pl.ds(pl.multiple_of(qi * blk, blk), blk)
    q = _rope_pair(q_ref[...], cos_ref[rows_q, :], sin_ref[rows_q, :])
    lane = lax.broadcasted_iota(jnp.int32, q.shape, 1)
    kmean = kmean_scr[...]
    kidx = lax.broadcasted_iota(jnp.int32, (blk, blk), 0)
    qidx = lax.broadcasted_iota(jnp.int32, (blk, blk), 1)
    heads = range(2)
    hb = [(h, b) for h in heads for b in range(2)]
    qh = [jnp.where(lane // HEAD_DIM == h, q, 0.0) for h in heads]
    gate = [lax.dot_general(kmean, qh[h], (((1,), (1,)), ((), ())), precision=lax.Precision.HIGHEST,
                            preferred_element_type=F32) for h in heads]
    for h in heads:
        bias_scr[h] = _topk_bias(gate[h], qi)
    qs = [(qh[h] * (HEAD_DIM ** -0.5 * LOG2_E)).astype(BF16) for h in heads]
    n_steps = (qi + 1) // 2

    def step_blocks(t):
        return [jnp.minimum(2 * t + b, nb - 1) for b in range(2)]

    def store_logits(slot, t):
        js = step_blocks(t)
        for i, (h, b) in enumerate(hb):
            s_buf[slot, i] = _dot_nt(kbf_scr[js[b]], qs[h])

    def weighted_values(slot, js):
        return [_dot(vt_scr[js[b], h], p_buf[slot, i]) for i, (h, b) in enumerate(hb)]

    def accumulate(acc, alpha, pv):
        return [alpha[h] * acc[h] + (pv[2 * h] + pv[2 * h + 1]) for h in heads]

    def attend_past(p_own, probabilities, state0):
        zeros_p = jnp.zeros((blk, blk), BF16)
        for i, x in enumerate((p_own[0], zeros_p, p_own[1], zeros_p)):
            p_buf[0, i] = x
        store_logits(0, 0)

        def half(t, slot, carry):
            state, acc, alpha_prev, j0, j1 = carry
            store_logits(1 - slot, t + 1)
            pv = weighted_values(slot, (j0, j1))
            js = step_blocks(t)
            bias = [bias_scr[h, pl.ds(js[b], 1), :] for h, b in hb]
            p, state, alpha = probabilities(slot, bias, state)
            for i in range(4):
                p_buf[1 - slot, i] = p[i]
            return state, accumulate(acc, alpha_prev, pv), alpha, js[0], js[1]

        def trip(u, carry):
            return half(2 * u + 1, 1, half(2 * u, 0, carry))

        state, acc, alpha, j0, j1 = lax.fori_loop(0, (n_steps + 1) // 2, trip, (state0, acc0, ones, qi, qi))
        for h, x in enumerate(accumulate(acc, alpha, weighted_values(0, (j0, j1)))):
            acc_scr[h] = x

    k_own = kbf_scr[qi]
    s_own = [jnp.where(kidx <= qidx, _dot_nt(k_own, qs[h]), NEG_INF) for h in heads]
    m_own = [jnp.max(s_own[h], axis=0, keepdims=True) for h in heads]
    acc0 = [jnp.zeros((VT_ROWS, blk), F32) for h in heads]
    ones = [jnp.ones((1, blk), F32) for h in heads]

    ones8 = jnp.where(lax.broadcasted_iota(jnp.int32, (8, PAIR_W), 1) // HEAD_DIM
                      == lax.broadcasted_iota(jnp.int32, (8, PAIR_W), 0) % 2, 1.0, 0.0)
    q_sq = jnp.square(qs[0].astype(F32) + qs[1].astype(F32))
    qn2 = lax.dot_general(ones8, q_sq, (((1,), (1,)), ((), ())), precision=lax.Precision.HIGHEST,
                          preferred_element_type=F32)
    kn2 = knorm_scr[...]
    k_lane = lax.broadcasted_iota(jnp.int32, kn2.shape, 1)
    ref = []
    for h in heads:
        kn2_h = jnp.max(jnp.where(k_lane // HEAD_DIM == h, kn2, 0.0), axis=1, keepdims=True)
        ref.append(jnp.sqrt(qn2[h:h + 1] * kn2_h) * BOUND_SLACK)
    gap = jnp.maximum(jnp.max(ref[0] - m_own[0]), jnp.max(ref[1] - m_own[1]))
    single_pass = gap <= EXP_RANGE_LOG2

    @pl.when(single_pass)
    def _():
        def probabilities(slot, bias, state):
            p = [jnp.exp2(s_buf[slot, i] + (bias[i] - ref[h])).astype(BF16) for i, (h, b) in enumerate(hb)]
            return p, state, ones

        attend_past([jnp.exp2(s_own[h] - ref[h]).astype(BF16) for h in heads], probabilities, ())

    @pl.when(jnp.logical_not(single_pass))
    def _():
        def probabilities(slot, bias, m):
            bm = [jnp.max(s_buf[slot, i], axis=0, keepdims=True) + bias[i] for i in range(4)]
            m_new = [jnp.maximum(m[h], jnp.maximum(bm[2 * h], bm[2 * h + 1])) for h in heads]
            alpha = [jnp.exp2(m[h] - m_new[h]) for h in heads]
            p = [jnp.exp2(s_buf[slot, i] + (bias[i] - m_new[h])).astype(BF16) for i, (h, b) in enumerate(hb)]
            return p, m_new, alpha

        attend_past([jnp.exp2(s_own[h] - m_own[h]).astype(BF16) for h in heads], probabilities, m_own)

    out = [acc_scr[h, :HEAD_DIM, :] / acc_scr[h, HEAD_DIM:HEAD_DIM + 1, :] for h in heads]
    o_ref[...] = jnp.concatenate(out, axis=0).T.astype(o_ref.dtype)


def _moba_prompt(qkv, cos_t, sin_t, n_seq, seq_len):
    proj3 = qkv.reshape(n_seq, seq_len, 3 * D_MODEL)
    nb = seq_len // MOBA_BLOCK
    qb, kb, vb = 0, N_PAIRS, 2 * N_PAIRS
    seq_spec = lambda col0: pl.BlockSpec((None, seq_len, PAIR_W), lambda b, hp, qi: (b, 0, col0 + hp))
    tab_spec = pl.BlockSpec((seq_len, PAIR_W), lambda b, hp, qi: (0, 0))
    return pl.pallas_call(
        _moba_prompt_kernel,
        out_shape=(
            jax.ShapeDtypeStruct((n_seq, seq_len, D_MODEL), BF16),
            jax.ShapeDtypeStruct((n_seq, seq_len, D_MODEL), F32),
            jax.ShapeDtypeStruct((n_seq, seq_len, D_MODEL), F32),
        ),
        grid=(n_seq, N_PAIRS, nb),
        in_specs=[
            pl.BlockSpec((None, MOBA_BLOCK, PAIR_W), lambda b, hp, qi: (b, qi, qb + hp)),
            seq_spec(kb),
            seq_spec(vb),
            tab_spec,
            tab_spec,
        ],
        out_specs=(
            pl.BlockSpec((None, MOBA_BLOCK, PAIR_W), lambda b, hp, qi: (b, qi, hp)),
            seq_spec(0),
            seq_spec(0),
        ),
        scratch_shapes=[
            pltpu.VMEM((nb, MOBA_BLOCK, PAIR_W), BF16),
            pltpu.VMEM((nb, 2, VT_ROWS, MOBA_BLOCK), BF16),
            pltpu.VMEM((nb, PAIR_W), F32),
            pltpu.VMEM((2, nb, MOBA_BLOCK), F32),
            pltpu.VMEM((1, PAIR_W), F32),
            pltpu.VMEM((2, VT_ROWS, MOBA_BLOCK), F32),
            pltpu.VMEM((2, 4, MOBA_BLOCK, MOBA_BLOCK), F32),
            pltpu.VMEM((2, 4, MOBA_BLOCK, MOBA_BLOCK), BF16),
        ],
        compiler_params=_cparams("parallel", "parallel", "arbitrary"),
        name="moba_prompt",
    )(proj3, proj3, proj3, cos_t, sin_t)


def _hi_lo(x):
    hi = x.astype(BF16)
    return hi, (x - hi.astype(F32)).astype(BF16)


def _mm3(dot_fn, a, b):
    ah, al = _hi_lo(a)
    bh, bl = _hi_lo(b)
    return dot_fn(ah, bh) + dot_fn(ah, bl) + dot_fn(al, bh)


def _mm1(dot_fn, a, b):
    return dot_fn(a.astype(BF16), b.astype(BF16))


_INV_MM = _mm1


def _head_ones3():
    r = lax.broadcasted_iota(jnp.int32, (3 * PAIR_W, PAIR_W), 0)
    c = lax.broadcasted_iota(jnp.int32, (3 * PAIR_W, PAIR_W), 1)
    return jnp.where((r % PAIR_W) // HEAD_DIM == c // HEAD_DIM, 1.0, 0.0).astype(BF16)


def _head_sums(x, ones3):
    hi = x.astype(BF16)
    r1 = x - hi.astype(F32)
    mid = r1.astype(BF16)
    lo = (r1 - mid.astype(F32)).astype(BF16)
    return _dot(jnp.concatenate([hi, mid, lo], axis=1), ones3)


LORA_W = DECAY_LORA + AAA_LORA


def _split3(x):
    hi = x.astype(BF16)
    r1 = x - hi.astype(F32)
    mid = r1.astype(BF16)
    return hi, mid, (r1 - mid.astype(F32)).astype(BF16)


def _chunk_decay_sums(lw, chunk, n_dec):
    tm = lw.shape[0]
    ci = lax.broadcasted_iota(jnp.int32, (tm, tm), 0)
    cj = lax.broadcasted_iota(jnp.int32, (tm, tm), 1)
    same = ci // chunk == cj // chunk
    di = lax.broadcasted_iota(jnp.int32, (n_dec, tm), 0)
    dj = lax.broadcasted_iota(jnp.int32, (n_dec, tm), 1)
    sel = jnp.concatenate([
        jnp.where(same & (ci >= cj), 1.0, 0.0),
        jnp.where(same, 1.0, 0.0),
        jnp.where(di == dj // chunk, 1.0, 0.0)], axis=0).astype(BF16)
    big = _dot(sel, jnp.concatenate(_split3(lw), axis=1))
    big = big[:, :D_MODEL] + big[:, D_MODEL:2 * D_MODEL] + big[:, 2 * D_MODEL:]
    return big[:tm], big[tm:2 * tm], big[2 * tm:]


def _rwkv_prep_kernel(p_ref, above_ref, ovr_ref, mu_ref, w0_ref, w2_ref, a0_ref, a2_ref, g2_ref, kk_ref, ka_ref,
                      rk_ref, kap_o, r_o, kt_o, bt_o, kh_o, bh_o, v_o, dec_o, g_o, bonus_o,
                      *, period, chunk, tiles_per_seq):
    p = p_ref[...]
    tm = p.shape[0]
    row = lax.broadcasted_iota(jnp.int32, (tm, 1), 0)
    if tiles_per_seq is None:
        shifted_in = ovr_ref[...]
    else:
        first = pl.program_id(0) % tiles_per_seq == 0
        shifted_in = jnp.where(first, ovr_ref[...], above_ref[SUBLANES - 1:SUBLANES, :])
    prev = jnp.where(row % period == 0, shifted_in, pltpu.roll(p, 1, 0))
    xm = p + (prev - p) * mu_ref[...]
    r = xm[:, :D_MODEL]
    k = xm[:, D_MODEL:2 * D_MODEL]
    v = xm[:, 2 * D_MODEL:3 * D_MODEL]
    lora = xm[:, 3 * D_MODEL:3 * D_MODEL + LORA_W]
    xg = xm[:, 3 * D_MODEL + LORA_W:]
    z = w0_ref[...] + jnp.dot(jnp.tanh(lora), w2_ref[...], precision=lax.Precision.HIGHEST,
                              preferred_element_type=F32)
    w_log = -jax.nn.softplus(-z) - 0.5
    a = jax.nn.sigmoid(a0_ref[...] + _dot(lora.astype(BF16), a2_ref[...]))
    g_o[...] = _dot(jax.nn.sigmoid(xg).astype(BF16), g2_ref[...])
    kk = k * kk_ref[...]
    k2 = k * (1.0 + (a - 1.0) * ka_ref[...])
    rk = r * k2 * rk_ref[...]
    lw = -jnp.exp(w_log)
    cum, tot, dec = _chunk_decay_sums(lw, chunk, dec_o.shape[0])
    dec_o[...] = jnp.exp(dec)
    g_in = jnp.exp(cum)
    g_ex = jnp.exp(cum - lw)
    g_inv = jnp.exp(-cum)
    g_end = jnp.exp(tot - cum)
    ones3 = _head_ones3()
    for pr in range(N_PAIRS):
        sl = slice(pr * PAIR_W, (pr + 1) * PAIR_W)
        kks = kk[:, sl]
        kap = kks / jnp.maximum(jnp.sqrt(_head_sums(kks * kks, ones3)), 1e-12)
        b = kap * a[:, sl]
        kap_o[pr] = (kap * g_ex[:, sl]).astype(BF16)
        r_o[pr] = r[:, sl] * g_in[:, sl]
        kt_o[pr] = (k2[:, sl] * g_inv[:, sl]).astype(BF16)
        bt_o[pr] = (b * g_inv[:, sl]).astype(BF16)
        kh_o[pr] = (k2[:, sl] * g_end[:, sl]).astype(BF16)
        bh_o[pr] = (b * g_end[:, sl]).astype(BF16)
        v_o[pr] = v[:, sl].astype(BF16)
        bonus_o[:, sl] = _head_sums(rk[:, sl], ones3) * v[:, sl]


def _rwkv_prep(p_rwkv, ovr, rw, *, tm, period, chunk):
    rows = p_rwkv.shape[0]
    n_dec = max(8, tm // chunk)
    if ovr.shape[1] == 1:
        tiles_per_seq = rows // ovr.shape[0] // tm
        assert period == tm
        ovr_spec = pl.BlockSpec((None, 1, SHIFT_W), lambda i: (i // tiles_per_seq, 0, 0))
    else:
        tiles_per_seq = None
        ovr_spec = pl.BlockSpec((None, tm, SHIFT_W), lambda i: (0, i, 0))
    above_spec = pl.BlockSpec((SUBLANES, SHIFT_W), lambda i: (jnp.maximum(i * (tm // SUBLANES) - 1, 0), 0))
    full = lambda a: pl.BlockSpec(a.shape, lambda i: (0,) * a.ndim)
    pm = lambda dt: jax.ShapeDtypeStruct((N_PAIRS, rows, PAIR_W), dt)
    rm = jax.ShapeDtypeStruct((rows, D_MODEL), F32)
    pm_spec = pl.BlockSpec((N_PAIRS, tm, PAIR_W), lambda i: (0, i, 0))
    rm_spec = pl.BlockSpec((tm, D_MODEL), lambda i: (i, 0))
    consts = (rw["mu"], rw["w0"], rw["w2"], rw["a0"], rw["a2"], rw["g2"], rw["k_k"], rw["k_a"], rw["r_k"])
    outs = pl.pallas_call(
        functools.partial(_rwkv_prep_kernel, period=period, chunk=chunk, tiles_per_seq=tiles_per_seq),
        out_shape=(pm(BF16), pm(F32)) + (pm(BF16),) * 5
        + (jax.ShapeDtypeStruct((rows // tm, n_dec, D_MODEL), F32), rm, rm),
        grid=(rows // tm,),
        in_specs=[pl.BlockSpec((tm, SHIFT_W), lambda i: (i, 0)), above_spec, ovr_spec]
        + [full(a) for a in consts],
        out_specs=(pm_spec,) * 7 + (pl.BlockSpec((None, n_dec, D_MODEL), lambda i: (i, 0, 0)), rm_spec, rm_spec),
        compiler_params=_cparams("parallel"),
        name="rwkv_prep",
    )(p_rwkv, p_rwkv, ovr, *consts)
    dec = outs[7][:, :tm // chunk].reshape(rows // chunk, 1, D_MODEL)
    return outs[:7], dec, outs[8], outs[9]


def _rwkv_weights(mu_shift, w0, w2_decay, a0, a2, g2, k_k, k_a, r_k):
    zeros = jnp.zeros((DECAY_LORA, D_MODEL), F32)
    return dict(
        mu=mu_shift.reshape(1, SHIFT_W), w0=w0.reshape(1, D_MODEL), a0=a0.reshape(1, D_MODEL),
        w2=jnp.concatenate([w2_decay, zeros], axis=0),
        a2=jnp.concatenate([zeros, a2], axis=0).astype(BF16),
        g2=g2.astype(BF16), k_k=k_k.reshape(1, D_MODEL), k_a=k_a.reshape(1, D_MODEL),
        r_k=r_k.reshape(1, D_MODEL))


INV_BASE = 8


def _unit_lower_inverses(l_ms, ti, tj):
    cl = l_ms[0].shape[0]
    base = min(INV_BASE, cl)
    eye = jnp.where(ti == tj, 1.0, 0.0)
    diag_blk = ti // base == tj // base
    l_bs = [jnp.where(diag_blk, l_m, 0.0) for l_m in l_ms]
    t_ms = [eye - l_b for l_b in l_bs]
    for _ in range(base.bit_length() - 2):
        l_bs = [_INV_MM(_dot, l_b, l_b) for l_b in l_bs]
        t_ms = [t_m + _INV_MM(_dot, t_m, l_b) for t_m, l_b in zip(t_ms, l_bs)]
    size = base
    while size < cl:
        lower_left = (ti // (2 * size) == tj // (2 * size)) & (ti // size != tj // size)
        tmp = [_INV_MM(_dot, t_m, jnp.where(lower_left, l_m, 0.0)) for t_m, l_m in zip(t_ms, l_ms)]
        t_ms = [t_m - _INV_MM(_dot, x, t_m) for t_m, x in zip(t_ms, tmp)]
        size *= 2
    return t_ms


def _rwkv_chunk_kernel(kap_ref, r_ref, kt_ref, bt_ref, kh_ref, bh_ref, v_ref, dec_ref, s0_ref, o_ref, so_ref,
                       st_scr):
    c = pl.program_id(1)
    n_c = pl.num_programs(1)
    cl = r_ref.shape[1]
    hd = HEAD_DIM

    @pl.when(c == 0)
    def _():
        z = jnp.zeros((hd, hd), F32)
        for pr in range(N_PAIRS):
            top = jnp.concatenate([s0_ref[2 * pr], z], axis=1)
            bot = jnp.concatenate([z, s0_ref[2 * pr + 1]], axis=1)
            st_scr[pr] = jnp.concatenate([top, bot], axis=0)

    ti = lax.broadcasted_iota(jnp.int32, (cl, cl), 0)
    tj = lax.broadcasted_iota(jnp.int32, (cl, cl), 1)
    tril = ti >= tj
    stril = ti > tj
    head1 = lax.broadcasted_iota(jnp.int32, (cl, PAIR_W), 1) >= hd
    sr = lax.broadcasted_iota(jnp.int32, (PAIR_W, PAIR_W), 0)
    sc = lax.broadcasted_iota(jnp.int32, (PAIR_W, PAIR_W), 1)
    same_head = (sr // hd) == (sc // hd)

    pairs = range(N_PAIRS)
    hs = [(pr, h) for pr in pairs for h in range(2)]
    bf = lambda xs: [x.astype(BF16) for x in xs]
    r_t = [r_ref[pr] for pr in pairs]
    kap_b = [kap_ref[pr] for pr in pairs]
    kt_b = [kt_ref[pr] for pr in pairs]
    bt_b = [bt_ref[pr] for pr in pairs]
    v_b = [v_ref[pr] for pr in pairs]
    masks = (jnp.logical_not(head1), head1)
    r_h = [jnp.where(masks[h], r_t[pr], 0.0) for pr, h in hs]
    r_hb = bf(r_h)
    kap_hb = [jnp.where(masks[h], kap_b[pr], jnp.zeros_like(kap_b[pr])) for pr, h in hs]
    l_m = [jnp.where(stril, _dot_nt(kap_hb[i], bt_b[pr]), 0.0) for i, (pr, h) in enumerate(hs)]
    a_kk = [jnp.where(stril, _dot_nt(kap_hb[i], kt_b[pr]), 0.0) for i, (pr, h) in enumerate(hs)]
    a_rk = [jnp.where(tril, _dot_nt(r_hb[i], kt_b[pr]), 0.0) for i, (pr, h) in enumerate(hs)]
    a_rb = [jnp.where(tril, _dot_nt(r_hb[i], bt_b[pr]), 0.0) for i, (pr, h) in enumerate(hs)]
    av = [_dot(a, v_b[pr]) for a, (pr, h) in zip(bf(a_kk), hs)]
    oi_h = [_dot(a, v_b[pr]) for a, (pr, h) in zip(bf(a_rk), hs)]
    t_b = bf(_unit_lower_inverses(l_m, ti, tj))
    a_rb_b = bf(a_rb)
    w_h = [_dot(t, k) for t, k in zip(t_b, kap_hb)]
    u_h = [_dot(t, x) for t, x in zip(t_b, bf(av))]
    rp_h = [r - _dot(a, w) for r, a, w in zip(r_h, a_rb_b, bf(w_h))]
    oi_h = [o - _dot(a, u) for o, a, u in zip(oi_h, a_rb_b, bf(u_h))]
    w_m = bf([w_h[2 * pr] + w_h[2 * pr + 1] for pr in pairs])
    u_m = bf([jnp.where(head1, u_h[2 * pr + 1], u_h[2 * pr]) for pr in pairs])
    rp = [rp_h[2 * pr] + rp_h[2 * pr + 1] for pr in pairs]
    oi = [jnp.where(head1, oi_h[2 * pr + 1], oi_h[2 * pr]) for pr in pairs]
    bh_b = [bh_ref[pr] for pr in pairs]
    kh_b = [kh_ref[pr] for pr in pairs]
    wtb = [jnp.where(same_head, _dot_tn(w_m[pr], bh_b[pr]), 0.0) for pr in pairs]
    n_t = [jnp.where(same_head, _dot_tn(v_b[pr], kh_b[pr]) - _dot_tn(u_m[pr], bh_b[pr]), 0.0) for pr in pairs]
    s = [st_scr[pr] for pr in pairs]
    o_new = [_mm3(_dot_nt, rp[pr], s[pr]) + oi[pr] for pr in pairs]
    s_new = [s[pr] * dec_ref[:, pr * PAIR_W:(pr + 1) * PAIR_W] - _mm3(_dot, s[pr], wtb[pr]) + n_t[pr]
             for pr in pairs]
    for pr in pairs:
        o_ref[pr] = o_new[pr]
        st_scr[pr] = s_new[pr]

    @pl.when(c == n_c - 1)
    def _():
        for pr in range(N_PAIRS):
            s = st_scr[pr]
            so_ref[2 * pr] = s[:hd, :hd]
            so_ref[2 * pr + 1] = s[hd:, hd:]


def _rwkv_chunks(pm_inputs, dec, s0, *, n_seq, chunk):
    rows = pm_inputs[0].shape[1]
    n_c = rows // n_seq // chunk
    pm_spec = pl.BlockSpec((N_PAIRS, chunk, PAIR_W), lambda s, c: (0, s * n_c + c, 0))
    st_spec = pl.BlockSpec((None, N_HEADS, HEAD_DIM, HEAD_DIM), lambda s, c: (s, 0, 0, 0))
    return pl.pallas_call(
        _rwkv_chunk_kernel,
        out_shape=(jax.ShapeDtypeStruct((N_PAIRS, rows, PAIR_W), F32),
                   jax.ShapeDtypeStruct((n_seq, N_HEADS, HEAD_DIM, HEAD_DIM), F32)),
        grid=(n_seq, n_c),
        in_specs=[pm_spec] * 7 + [pl.BlockSpec((None, 1, D_MODEL), lambda s, c: (s * n_c + c, 0, 0)), st_spec],
        out_specs=(pm_spec, st_spec),
        scratch_shapes=[pltpu.VMEM((N_PAIRS, PAIR_W, PAIR_W), F32)],
        compiler_params=_cparams("parallel", "arbitrary"),
        name="rwkv_chunks",
    )(*pm_inputs, dec, s0)


def _mixer_out_kernel(x_ref, gt_ref, npost_ref, o_ref, bonus_ref, g_ref, attn_ref, ga_ref, gb_ref,
                      lnw_ref, lnb_ref, wor_ref, wom_ref, wout_ref, y_ref):
    ones3 = _head_ones3()
    parts = []
    for pr in range(N_PAIRS):
        o = o_ref[pr]
        mu = _head_sums(o, ones3) * (1.0 / HEAD_DIM)
        d = o - mu
        var = _head_sums(d * d, ones3) * (1.0 / HEAD_DIM)
        parts.append(d * lax.rsqrt(var + LNX_EPS))
    on = jnp.concatenate(parts, axis=1) * lnw_ref[...] + lnb_ref[...]
    ya = _dot(((on + bonus_ref[...]) * g_ref[...]).astype(BF16), wor_ref[...])
    yb = _dot(attn_ref[...], wom_ref[...])
    y = jax.nn.sigmoid(ga_ref[...]) * ya + jax.nn.sigmoid(gb_ref[...]) * yb
    y = _dot(y.astype(BF16), wout_ref[...])
    y_ref[...] = x_ref[...] + gt_ref[...] * _rms(y, npost_ref[...])


def _mixer_out(x, mod, which, n_post, o_pm, bonus, g, attn, gates, lnx_w, lnx_b, w_o_rwkv, w_o_moba, w_out,
               *, tm, tiles_per_group):
    rows = x.shape[0]
    row_spec = pl.BlockSpec((tm, D_MODEL), lambda i: (i, 0))
    w_spec = pl.BlockSpec((D_MODEL, D_MODEL), lambda i: (0, 0))
    return pl.pallas_call(
        _mixer_out_kernel,
        out_shape=jax.ShapeDtypeStruct((rows, D_MODEL), F32),
        grid=(rows // tm,),
        in_specs=[
            row_spec,
            _mod_spec(mod, which, tm, tiles_per_group),
            _vec_spec(),
            pl.BlockSpec((N_PAIRS, tm, PAIR_W), lambda i: (0, i, 0)),
            row_spec, row_spec, row_spec,
            pl.BlockSpec((tm, D_MODEL), lambda i: (i, 0)),
            pl.BlockSpec((tm, D_MODEL), lambda i: (i, 1)),
            _vec_spec(), _vec_spec(), w_spec, w_spec, w_spec,
        ],
        out_specs=row_spec,
        compiler_params=_cparams("parallel"),
        name="mixer_out",
    )(x, mod, n_post, o_pm, bonus, g, attn, gates, gates, lnx_w, lnx_b, w_o_rwkv, w_o_moba, w_out)


def _rope_rows_kernel(qkv_ref, cos_ref, sin_ref, q_ref, k_ref):
    cos = cos_ref[...]
    sin = sin_ref[...]
    for pr in range(N_PAIRS):
        sl = slice(pr * PAIR_W, (pr + 1) * PAIR_W)
        q_ref[:, sl] = _rope_pair(qkv_ref[:, sl], cos, sin)
        k_ref[:, sl] = _rope_pair(qkv_ref[:, D_MODEL + pr * PAIR_W:D_MODEL + (pr + 1) * PAIR_W], cos, sin)


def _rope_rows(qkv, cos_t, sin_t):
    rows = qkv.shape[0]
    out = jax.ShapeDtypeStruct((rows, D_MODEL), F32)
    return pl.pallas_call(_rope_rows_kernel, out_shape=(out, out), name="rope_rows")(qkv, cos_t, sin_t)


TOK_SLOTS = 8


PAGES_PER_STEP = 8
PAGES_PER_BLOCK = MOBA_BLOCK // PAGE_SIZE


def _moba_sample_kernel(pt_ref, q_ref, kn_ref, vn_ref, *refs, n_seq, n_pages, n_tok):
    pps = PAGES_PER_STEP
    kp_refs, vp_refs = refs[:pps], refs[pps:2 * pps]
    o_ref, q2_scr, s_scr, pn_scr, acc_scr, l_scr = refs[2 * pps:]
    b = pl.program_id(0)
    n = pl.program_id(1)
    n_steps = pl.num_programs(1)
    rows = N_HEADS * TOK_SLOTS
    n_blk = n_pages // PAGES_PER_BLOCK
    blk_per_step = pps // PAGES_PER_BLOCK
    scale = HEAD_DIM ** -0.5
    row_head = lax.broadcasted_iota(jnp.int32, (rows, D_MODEL), 0) // TOK_SLOTS
    lane_head = lax.broadcasted_iota(jnp.int32, (rows, D_MODEL), 1) // HEAD_DIM
    slot_k = b % 2
    slot_v = (b + 1) % 2

    @pl.when((n == 0) & (b < n_seq))
    def _():
        q_bd = jnp.where(row_head == lane_head, jnp.tile(q_ref[...], (N_HEADS, 1)), 0.0)
        hi, lo = _hi_lo(q_bd)
        q2_scr[:rows] = hi
        q2_scr[rows:] = lo

    @pl.when(b < n_seq)
    def _():
        kh, kl = _hi_lo(jnp.concatenate([r[...] for r in kp_refs], axis=1))
        top = _dot(q2_scr[...], kh)
        s = top[:rows] + top[rows:] + _dot(q2_scr[:rows], kl)
        for i in range(blk_per_step):
            s_scr[slot_k, n * blk_per_step + i] = s[:, i * MOBA_BLOCK:(i + 1) * MOBA_BLOCK]

    @pl.when((n == n_steps - 1) & (b < n_seq))
    def _():
        cols = [jnp.sum(s_scr[slot_k, i], axis=1, keepdims=True) for i in range(n_blk)]
        gate = jnp.concatenate(cols, axis=1)
        col = lax.broadcasted_iota(jnp.int32, gate.shape, 1)
        cnt = jnp.zeros(gate.shape, jnp.int32)
        for n in range(n_blk):
            gn = gate[:, n:n + 1]
            cnt = cnt + jnp.where((gn > gate) | ((gn == gate) & (n < col)), 1, 0)
        bias = jnp.where(cnt < MOBA_TOPK, 0.0, NEG_INF)
        knh, knl = _hi_lo(kn_ref[...])
        qh = q2_scr[:rows]
        s_new = (_dot_nt(qh, knh) + _dot_nt(q2_scr[rows:], knh) + _dot_nt(qh, knl)) * scale
        t_q = lax.broadcasted_iota(jnp.int32, s_new.shape, 0) % TOK_SLOTS
        t_k = lax.broadcasted_iota(jnp.int32, s_new.shape, 1)
        s_new = jnp.where((t_k <= t_q) & (t_k < n_tok), s_new, NEG_INF)
        logits = [s_scr[slot_k, i] * scale + bias[:, i:i + 1] for i in range(n_blk)]
        mx = functools.reduce(jnp.maximum, logits)
        m = jnp.maximum(jnp.max(mx, axis=1, keepdims=True), jnp.max(s_new, axis=1, keepdims=True))
        p_new = jnp.exp(s_new - m)
        ps = [jnp.exp(x - m) for x in logits]
        for i in range(n_blk):
            s_scr[slot_k, i] = ps[i]
        tot = functools.reduce(jnp.add, ps)
        l_scr[slot_k] = jnp.sum(tot, axis=1, keepdims=True) + jnp.sum(p_new, axis=1, keepdims=True)
        pn_scr[slot_k] = p_new

    @pl.when(b >= 1)
    def _():
        @pl.when(n == 0)
        def _():
            acc_scr[...] = _dot(pn_scr[slot_v], vn_ref[...])

        for i in range(blk_per_step):
            vb = jnp.concatenate([r[...] for r in vp_refs[i * PAGES_PER_BLOCK:(i + 1) * PAGES_PER_BLOCK]], axis=1)
            acc_scr[...] += _dot_nt(s_scr[slot_v, n * blk_per_step + i].astype(BF16), vb.astype(BF16))

        @pl.when(n == n_steps - 1)
        def _():
            acc = jnp.where(row_head == lane_head, acc_scr[...] / l_scr[slot_v], 0.0)
            out = acc[:TOK_SLOTS]
            for h in range(1, N_HEADS):
                out = out + acc[h * TOK_SLOTS:(h + 1) * TOK_SLOTS]
            o_ref[...] = out


def _moba_sample(page_table, q_rot, k_new, v_new, cache_kt, cache_vt, *, n_tok):
    n_seq, n_pages = page_table.shape
    pps = PAGES_PER_STEP
    assert n_pages % pps == 0 and pps % PAGES_PER_BLOCK == 0 and n_tok <= TOK_SLOTS
    rows = N_HEADS * TOK_SLOTS
    n_blk = n_pages // PAGES_PER_BLOCK
    cur = lambda b: jnp.minimum(b, n_seq - 1)
    prev = lambda b: jnp.maximum(b - 1, 0)
    cur_spec = pl.BlockSpec((None, TOK_SLOTS, D_MODEL), lambda b, n, pt: (cur(b), 0, 0))
    prev_spec = pl.BlockSpec((None, TOK_SLOTS, D_MODEL), lambda b, n, pt: (prev(b), 0, 0))

    def page_spec(seq_of, i):
        return pl.BlockSpec((None, D_MODEL, PAGE_SIZE),
                            lambda b, n, pt: (pt[seq_of(b) * n_pages + n * pps + i], 0, 0))

    return pl.pallas_call(
        functools.partial(_moba_sample_kernel, n_seq=n_seq, n_pages=n_pages, n_tok=n_tok),
        out_shape=jax.ShapeDtypeStruct((n_seq, TOK_SLOTS, D_MODEL), F32),
        grid_spec=pltpu.PrefetchScalarGridSpec(
            num_scalar_prefetch=1,
            grid=(n_seq + 1, n_pages // pps),
            in_specs=[cur_spec, cur_spec, prev_spec]
            + [page_spec(cur, i) for i in range(pps)] + [page_spec(prev, i) for i in range(pps)],
            out_specs=prev_spec,
            scratch_shapes=[
                pltpu.VMEM((2 * rows, D_MODEL), BF16),
                pltpu.VMEM((2, n_blk, rows, MOBA_BLOCK), F32),
                pltpu.VMEM((2, rows, TOK_SLOTS), F32),
                pltpu.VMEM((rows, D_MODEL), F32),
                pltpu.VMEM((2, rows, 1), F32),
            ],
        ),
        compiler_params=_cparams("arbitrary", "arbitrary"),
        name="moba_sample",
    )(page_table.reshape(-1), q_rot, k_new, v_new, *([cache_kt] * pps), *([cache_vt] * pps))


DENSE_TM = 512
RWKV_TM = 256
PROMPT_CHUNK = 64
SAMPLE_CHUNK = 16


def _layer(x, mod, lw, *, tiles, shift_ovr, shift_period, prep_chunk, attend, rwkv_scan):
    tpg_dense, tpg_rwkv = tiles
    x1 = _ffn(x, mod, 0, lw["n1_pre"], lw["n1_post"], lw["w1_gu"], lw["w1_down"], tm=DENSE_TM,
              tiles_per_group=tpg_dense)
    p_rwkv, qkv, gates = _inproj(x1, mod, 3, lw["n2_pre"], lw["w_in"], tm=DENSE_TM, tiles_per_group=tpg_dense)
    attn, extras = attend(qkv)
    pm, dec, g, bonus = _rwkv_prep(p_rwkv, shift_ovr, lw["rwkv"], tm=RWKV_TM, period=shift_period,
                                   chunk=prep_chunk)
    o_pm, state = rwkv_scan(pm, dec)
    x2 = _mixer_out(x1, mod, 5, lw["n2_post"], o_pm, bonus, g, attn, gates, lw["lnx_w"], lw["lnx_b"],
                    lw["w_o_rwkv"], lw["w_o_moba"], lw["w_out"], tm=RWKV_TM, tiles_per_group=tpg_rwkv)
    y = _ffn(x2, mod, 6, lw["n3_pre"], lw["n3_post"], lw["w3_gu"], lw["w3_down"], tm=DENSE_TM,
             tiles_per_group=tpg_dense)
    return y, p_rwkv, qkv, state, extras


def kernel(x_prompt, x_sample, cache_k, cache_v, state_rwkv, state_shift, page_table, c_prompt, c_sample, w_ada, b_ada, n1_pre, n1_post, w1_gu, w1_down, n2_pre, n2_post, w_in, mu_shift, w0, w2_decay, a0, a2, g2, k_k, k_a, r_k, lnx_w, lnx_b, w_o_rwkv, w_o_moba, w_out, n3_pre, n3_post, w3_gu, w3_down):
    assert w_ada.shape[0] == 1, "single layer"
    bp, seq, _ = x_prompt.shape
    bs, ts, _ = x_sample.shape
    n_pool = cache_k.shape[1]
    n_pages = page_table.shape[1]
    rows_p, rows_s = bp * seq, bs * ts
    assert seq % DENSE_TM == 0 and seq % MOBA_BLOCK == 0 and rows_s % DENSE_TM == 0 and ts <= SAMPLE_CHUNK

    lw = dict(
        n1_pre=n1_pre, n1_post=n1_post, n2_pre=n2_pre, n2_post=n2_post, n3_pre=n3_pre, n3_post=n3_post,
        w1_gu=w1_gu[0].astype(BF16), w1_down=w1_down[0].astype(BF16),
        w3_gu=w3_gu[0].astype(BF16), w3_down=w3_down[0].astype(BF16),
        w_in=_split_w_in(w_in[0]),
        rwkv=_rwkv_weights(mu_shift[0], w0[0], w2_decay[0], a0[0], a2[0], g2[0], k_k[0], k_a[0], r_k[0]),
        lnx_w=lnx_w, lnx_b=lnx_b,
        w_o_rwkv=w_o_rwkv[0].astype(BF16), w_o_moba=w_o_moba[0].astype(BF16), w_out=w_out[0].astype(BF16),
    )

    n_c = bp + bs
    c_all = jnp.concatenate([c_prompt, c_sample, jnp.zeros((-n_c % 8, D_MODEL), F32)], axis=0)
    mod = _ada(c_all, w_ada[0], b_ada[0])
    mod_p = mod[:bp].reshape(bp, N_ADA, 1, D_MODEL).transpose(1, 0, 2, 3)
    mod_s = jnp.repeat(mod[bp:n_c].reshape(bs, N_ADA, D_MODEL), ts, axis=0).transpose(1, 0, 2)[:, None]

    cos_p, sin_p = _rope_tables(jnp.arange(seq, dtype=jnp.int32))

    def attend_p(qkv):
        attn, k_rot, v = _moba_prompt(qkv, cos_p, sin_p, bp, seq)
        return attn.reshape(rows_p, D_MODEL), (k_rot, v)

    def scan_p(pm, dec):
        s0 = jnp.zeros((bp, N_HEADS, HEAD_DIM, HEAD_DIM), F32)
        return _rwkv_chunks(pm, dec, s0, n_seq=bp, chunk=PROMPT_CHUNK)

    y_p, p_p, _, state_p, (k_p, v_p) = _layer(
        x_prompt.reshape(rows_p, D_MODEL), mod_p, lw, tiles=(seq // DENSE_TM, seq // RWKV_TM),
        shift_ovr=jnp.zeros((bp, 1, SHIFT_W), F32), shift_period=RWKV_TM, prep_chunk=PROMPT_CHUNK,
        attend=attend_p, rwkv_scan=scan_p)

    pos_s = n_pages * PAGE_SIZE + jnp.arange(ts, dtype=jnp.int32)
    cos_s, sin_s = (jnp.tile(t, (bs, 1)) for t in _rope_tables(pos_s))
    pad_tok = lambda a: jnp.pad(a.reshape(bs, ts, D_MODEL), ((0, 0), (0, TOK_SLOTS - ts), (0, 0)))
    cache_kt = cache_k[0].transpose(0, 2, 3, 1).reshape(n_pool, D_MODEL, PAGE_SIZE)
    cache_vt = cache_v[0].transpose(0, 2, 3, 1).reshape(n_pool, D_MODEL, PAGE_SIZE)

    def attend_s(qkv):
        q_rot, k_rot = _rope_rows(qkv, cos_s, sin_s)
        v = qkv[:, 2 * D_MODEL:]
        attn = _moba_sample(page_table, pad_tok(q_rot), pad_tok(k_rot), pad_tok(v), cache_kt, cache_vt, n_tok=ts)
        return attn[:, :ts].reshape(rows_s, D_MODEL).astype(BF16), (k_rot, v)

    def scan_s(pm, dec):
        pad = lambda a: jnp.pad(a.reshape(N_PAIRS, bs, ts, PAIR_W),
                                ((0, 0), (0, 0), (0, SAMPLE_CHUNK - ts), (0, 0))).reshape(N_PAIRS, -1, PAIR_W)
        o_pm, state = _rwkv_chunks([pad(a) for a in pm], dec, state_rwkv[0], n_seq=bs, chunk=SAMPLE_CHUNK)
        o_pm = o_pm.reshape(N_PAIRS, bs, SAMPLE_CHUNK, PAIR_W)[:, :, :ts].reshape(N_PAIRS, rows_s, PAIR_W)
        return o_pm, state

    y_s, p_s, _, state_s, (k_s, v_s) = _layer(
        x_sample.reshape(rows_s, D_MODEL), mod_s, lw, tiles=(1, 1),
        shift_ovr=jnp.repeat(state_shift[0], ts, axis=0)[None], shift_period=ts, prep_chunk=ts,
        attend=attend_s, rwkv_scan=scan_s)

    heads = lambda a, b, t: a.reshape(1, b, t, N_HEADS, HEAD_DIM)
    return (
        y_p.reshape(bp, seq, D_MODEL),
        y_s.reshape(bs, ts, D_MODEL),
        heads(k_p, bp, seq), heads(v_p, bp, seq),
        state_p[None], p_p.reshape(bp, seq, SHIFT_W)[:, -1][None],
        heads(k_s, bs, ts), heads(v_s, bs, ts),
        state_s[None], p_s.reshape(bs, ts, SHIFT_W)[:, -1][None],
    )
```

```python
import functools

import jax
import jax.numpy as jnp
from jax import lax
from jax.experimental import pallas as pl
from jax.experimental.pallas import tpu as pltpu

D_MODEL = 1024
HEAD_DIM = 64
N_HEADS = D_MODEL // HEAD_DIM
PAIR_W = 2 * HEAD_DIM
N_PAIRS = N_HEADS // 2
DECAY_LORA = 64
AAA_LORA = 64
GATE_LORA = 128
SHIFT_W = 3 * D_MODEL + DECAY_LORA + AAA_LORA + GATE_LORA
IN_W = SHIFT_W + 3 * D_MODEL + 2 * D_MODEL
D_FF = 11 * D_MODEL // 4
N_ADA = 9
MACARON_W = 0.5
RMS_EPS = 1e-6
LNX_EPS = 64e-5
MOBA_BLOCK = 256
MOBA_TOPK = 3
PAGE_SIZE = 128
ROPE_THETA = 10000.0

SUBLANES = 8
VMEM_LIMIT = 56 * 1024 * 1024

BF16 = jnp.bfloat16
F32 = jnp.float32
NEG_INF = float("-inf")


def _cparams(*sem):
    return pltpu.CompilerParams(dimension_semantics=sem, vmem_limit_bytes=VMEM_LIMIT)


def _rms(x, g):
    return x * lax.rsqrt(jnp.mean(x * x, axis=-1, keepdims=True) + RMS_EPS) * g


def _dot(a, b):
    return jnp.dot(a, b, preferred_element_type=F32)


def _dot_nt(a, b):
    return lax.dot_general(a, b, (((1,), (1,)), ((), ())), preferred_element_type=F32)


def _dot_tn(a, b):
    return lax.dot_general(a, b, (((0,), (0,)), ((), ())), preferred_element_type=F32)


ADA_TN = 1536


def _ada_kernel(c_ref, w_ref, b_ref, o_ref):
    c = c_ref[...]
    a = (c * jax.nn.sigmoid(c)).astype(BF16)
    o_ref[...] = _dot(a, w_ref[...].astype(BF16)) + b_ref[...]


def _ada(c_all, w_ada, b_ada):
    rows = c_all.shape[0]
    n_out = w_ada.shape[1]
    return pl.pallas_call(
        _ada_kernel,
        out_shape=jax.ShapeDtypeStruct((rows, n_out), F32),
        grid=(n_out // ADA_TN,),
        in_specs=[
            pl.BlockSpec((rows, D_MODEL), lambda j: (0, 0)),
            pl.BlockSpec((D_MODEL, ADA_TN), lambda j: (0, j)),
            pl.BlockSpec((1, ADA_TN), lambda j: (0, j)),
        ],
        out_specs=pl.BlockSpec((rows, ADA_TN), lambda j: (0, j)),
        compiler_params=_cparams("arbitrary"),
        name="ada_mod",
    )(c_all, w_ada, b_ada.reshape(1, n_out))


def _mod_spec(mod, which, tm, tiles_per_group):
    rows_in_group = mod.shape[2]
    if rows_in_group == 1:
        return pl.BlockSpec((None, None, 1, D_MODEL), lambda i, *_: (which, i // tiles_per_group, 0, 0))
    return pl.BlockSpec((None, None, tm, D_MODEL), lambda i, *_: (which, 0, i, 0))


def _vec_spec():
    return pl.BlockSpec((1, D_MODEL), lambda i, *_: (0, 0))


FFN_TF = D_FF // 2


def _ffn_kernel(x_ref, sh_ref, sc_ref, gt_ref, npre_ref, npost_ref, wg_ref, wu_ref, wd_ref, o_ref, h_scr, acc_scr):
    j = pl.program_id(1)

    @pl.when(j == 0)
    def _():
        h = _rms(x_ref[...], npre_ref[...]) * (1.0 + sc_ref[...]) + sh_ref[...]
        h_scr[...] = h.astype(BF16)
        acc_scr[...] = jnp.zeros_like(acc_scr)

    h = h_scr[...]
    g = _dot(h, wg_ref[...])
    u = _dot(h, wu_ref[...])
    a = (g * jax.nn.sigmoid(g) * u).astype(BF16)
    acc_scr[...] += _dot(a, wd_ref[...])

    @pl.when(j == pl.num_programs(1) - 1)
    def _():
        y = _rms(acc_scr[...], npost_ref[...])
        o_ref[...] = x_ref[...] + MACARON_W * gt_ref[...] * y


def _ffn(x, mod, which0, n_pre, n_post, wgu, wd, *, tm, tiles_per_group):
    rows = x.shape[0]
    nf = D_FF // FFN_TF
    row_spec = pl.BlockSpec((tm, D_MODEL), lambda i, j: (i, 0))
    return pl.pallas_call(
        _ffn_kernel,
        out_shape=jax.ShapeDtypeStruct((rows, D_MODEL), F32),
        grid=(rows // tm, nf),
        in_specs=[
            row_spec,
            _mod_spec(mod, which0, tm, tiles_per_group),
            _mod_spec(mod, which0 + 1, tm, tiles_per_group),
            _mod_spec(mod, which0 + 2, tm, tiles_per_group),
            _vec_spec(),
            _vec_spec(),
            pl.BlockSpec((D_MODEL, FFN_TF), lambda i, j: (0, j)),
            pl.BlockSpec((D_MODEL, FFN_TF), lambda i, j: (0, j + nf)),
            pl.BlockSpec((FFN_TF, D_MODEL), lambda i, j: (j, 0)),
        ],
        out_specs=row_spec,
        scratch_shapes=[pltpu.VMEM((tm, D_MODEL), BF16), pltpu.VMEM((tm, D_MODEL), F32)],
        compiler_params=_cparams("parallel", "arbitrary"),
        name="ffn",
    )(x, mod, mod, mod, n_pre, n_post, wgu, wgu, wd)


INPROJ_WIDTHS = (SHIFT_W, 3 * D_MODEL, 2 * D_MODEL)
INPROJ_SPLIT = 2


def _inproj_kernel(x_ref, sh_ref, sc_ref, npre_ref, w0_ref, w1_ref, w2_ref, o0_ref, o1_ref, o2_ref, h_scr):
    j = pl.program_id(1)

    @pl.when(j == 0)
    def _():
        h = _rms(x_ref[...], npre_ref[...]) * (1.0 + sc_ref[...]) + sh_ref[...]
        h_scr[...] = h.astype(BF16)

    for sec, (w_ref, o_ref) in enumerate(((w0_ref, o0_ref), (w1_ref, o1_ref), (w2_ref, o2_ref))):
        @pl.when(j // INPROJ_SPLIT == sec)
        def _(w_ref=w_ref, o_ref=o_ref):
            o_ref[...] = _dot(h_scr[...], w_ref[...])


def _inproj(x, mod, which0, n_pre, w_secs, *, tm, tiles_per_group):
    rows = x.shape[0]

    def sec_col(sec):
        return lambda i, j: jnp.clip(j - sec * INPROJ_SPLIT, 0, INPROJ_SPLIT - 1)

    w_specs, o_specs = [], []
    for sec, width in enumerate(INPROJ_WIDTHS):
        tn = width // INPROJ_SPLIT
        col = sec_col(sec)
        w_specs.append(pl.BlockSpec((D_MODEL, tn), lambda i, j, col=col: (0, col(i, j))))
        o_specs.append(pl.BlockSpec((tm, tn), lambda i, j, col=col: (i, col(i, j))))
    return pl.pallas_call(
        _inproj_kernel,
        out_shape=tuple(jax.ShapeDtypeStruct((rows, w), F32) for w in INPROJ_WIDTHS),
        grid=(rows // tm, len(INPROJ_WIDTHS) * INPROJ_SPLIT),
        in_specs=[
            pl.BlockSpec((tm, D_MODEL), lambda i, j: (i, 0)),
            _mod_spec(mod, which0, tm, tiles_per_group),
            _mod_spec(mod, which0 + 1, tm, tiles_per_group),
            _vec_spec(),
            *w_specs,
        ],
        out_specs=tuple(o_specs),
        scratch_shapes=[pltpu.VMEM((tm, D_MODEL), BF16)],
        compiler_params=_cparams("parallel", "arbitrary"),
        name="inproj",
    )(x, mod, mod, n_pre, *w_secs)


def _split_w_in(w_in):
    w = w_in.astype(BF16)
    c1 = SHIFT_W
    c2 = SHIFT_W + 3 * D_MODEL
    return w[:, :c1], w[:, c1:c2], w[:, c2:]


def _rope_tables(pos):
    half = HEAD_DIM // 2
    inv = ROPE_THETA ** (-jnp.arange(half, dtype=F32) / half)
    ang = pos.astype(F32)[:, None] * inv[None, :]
    cos, sin = jnp.cos(ang), jnp.sin(ang)
    return jnp.tile(cos, (1, 4)), jnp.tile(jnp.concatenate([-sin, sin], axis=1), (1, 2))


def _rope_pair(x, cos, sin_signed):
    lane = lax.broadcasted_iota(jnp.int32, x.shape, 1)
    first_half = (lane % HEAD_DIM) < HEAD_DIM // 2
    partner = jnp.where(first_half, pltpu.roll(x, PAIR_W - HEAD_DIM // 2, 1), pltpu.roll(x, HEAD_DIM // 2, 1))
    return x * cos + partner * sin_signed


LOG2_E = 1.4426950408889634
BLOCKS_PER_TRIP = 4
VT_ROWS = HEAD_DIM + 16


def _topk_bias(gate, n_past):
    nb = gate.shape[0]
    blk = lax.broadcasted_iota(jnp.int32, gate.shape, 0)
    cnt = jnp.zeros(gate.shape, jnp.int32)
    for jp in range(nb):
        gj = gate[jp:jp + 1, :]
        beats = (gj > gate) | ((gj == gate) & (jp < blk))
        cnt = cnt + jnp.where(beats & (jp < n_past), 1, 0)
    sel = (blk < n_past) & (cnt < MOBA_TOPK)
    return jnp.where(sel, 0.0, NEG_INF)


def _moba_prompt_kernel(q_ref, k_ref, v_ref, cos_ref, sin_ref, o_ref, ko_ref, vo_ref,
                        kbf_scr, vt_scr, kmean_scr, bias_scr):
    qi = pl.program_id(2)
    nb = kbf_scr.shape[0]
    blk = MOBA_BLOCK

    @pl.when(qi == 0)
    def _():
        def prep(j, carry):
            rows = pl.ds(pl.multiple_of(j * blk, blk), blk)
            kr = _rope_pair(k_ref[rows, :], cos_ref[rows, :], sin_ref[rows, :])
            ko_ref[rows, :] = kr
            kbf_scr[j] = kr.astype(BF16)
            kmean_scr[pl.ds(j, 1), :] = jnp.mean(kr, axis=0, keepdims=True)
            v = v_ref[rows, :]
            vo_ref[rows, :] = v
            vt = v.T.astype(BF16)
            for h in range(2):
                vt_scr[j, h, :HEAD_DIM, :] = vt[h * HEAD_DIM:(h + 1) * HEAD_DIM]
                vt_scr[j, h, HEAD_DIM:, :] = ones_rows
            return carry

        pad_row = lax.broadcasted_iota(jnp.int32, (VT_ROWS - HEAD_DIM, blk), 0)
        ones_rows = jnp.where(pad_row == 0, 1.0, 0.0).astype(BF16)
        lax.fori_loop(0, nb, prep, 0)

    rows_q = pl.ds(pl.multiple_of(qi * blk, blk), blk)
    q = _rope_pair(q_ref[...], cos_ref[rows_q, :], sin_ref[rows_q, :])
    lane = lax.broadcasted_iota(jnp.int32, q.shape, 1)
    kmean = kmean_scr[...]
    kidx = lax.broadcasted_iota(jnp.int32, (blk, blk), 0)
    qidx = lax.broadcasted_iota(jnp.int32, (blk, blk), 1)
    heads = range(2)
    hb = [(h, b) for h in heads for b in range(BLOCKS_PER_TRIP)]
    qh = [jnp.where(lane // HEAD_DIM == h, q, 0.0) for h in heads]
    gate = [lax.dot_general(kmean, qh[h], (((1,), (1,)), ((), ())), precision=lax.Precision.HIGHEST,
                            preferred_element_type=F32) for h in heads]
    for h in heads:
        bias = _topk_bias(gate[h], qi)
        for j in range(nb):
            bias_scr[h, j] = jnp.broadcast_to(bias[j:j + 1], (SUBLANES, blk))
    qs = [(qh[h] * (HEAD_DIM ** -0.5 * LOG2_E)).astype(BF16) for h in heads]

    def shifted_exp2(s, shift):
        x = s.reshape(blk // SUBLANES, SUBLANES, blk) + shift[None]
        return jnp.exp2(x).reshape(blk, blk).astype(BF16)

    k_own = kbf_scr[qi]
    s_own = [jnp.where(kidx <= qidx, _dot_nt(k_own, qs[h]), NEG_INF) for h in heads]
    m_own = [jnp.max(s_own[h], axis=0, keepdims=True) for h in heads]
    acc_own = [_dot(vt_scr[qi, h], jnp.exp2(s_own[h] - m_own[h]).astype(BF16)) for h in heads]

    def trip(u, carry):
        m, acc = carry
        js = [jnp.minimum(BLOCKS_PER_TRIP * u + b, nb - 1) for b in range(BLOCKS_PER_TRIP)]
        s = [_dot_nt(kbf_scr[js[b]], qs[h]) for h, b in hb]
        bias = [bias_scr[h, js[b]] for h, b in hb]
        bm = [jnp.max(x, axis=0, keepdims=True) + c[:1] for x, c in zip(s, bias)]
        m_new = [jnp.maximum(m[h], functools.reduce(jnp.maximum, bm[h * BLOCKS_PER_TRIP:(h + 1) * BLOCKS_PER_TRIP]))
                 for h in heads]
        alpha = [jnp.exp2(m[h] - m_new[h]) for h in heads]
        p = [shifted_exp2(x, c - m_new[h]) for x, c, (h, b) in zip(s, bias, hb)]
        pv = [_dot(vt_scr[js[b], h], x) for x, (h, b) in zip(p, hb)]
        acc = [alpha[h] * acc[h] + functools.reduce(jnp.add, pv[h * BLOCKS_PER_TRIP:(h + 1) * BLOCKS_PER_TRIP])
               for h in heads]
        return m_new, acc

    n_trips = (qi + BLOCKS_PER_TRIP - 1) // BLOCKS_PER_TRIP
    _, acc = lax.fori_loop(0, n_trips, trip, (m_own, acc_own))
    out = [acc[h][:HEAD_DIM] / acc[h][HEAD_DIM:HEAD_DIM + 1] for h in heads]
    o_ref[...] = jnp.concatenate(out, axis=0).T.astype(o_ref.dtype)


def _moba_prompt(qkv, cos_t, sin_t, n_seq, seq_len):
    proj3 = qkv.reshape(n_seq, seq_len, 3 * D_MODEL)
    nb = seq_len // MOBA_BLOCK
    qb, kb, vb = 0, N_PAIRS, 2 * N_PAIRS
    seq_spec = lambda col0: pl.BlockSpec((None, seq_len, PAIR_W), lambda b, hp, qi: (b, 0, col0 + hp))
    tab_spec = pl.BlockSpec((seq_len, PAIR_W), lambda b, hp, qi: (0, 0))
    return pl.pallas_call(
        _moba_prompt_kernel,
        out_shape=(
            jax.ShapeDtypeStruct((n_seq, seq_len, D_MODEL), BF16),
            jax.ShapeDtypeStruct((n_seq, seq_len, D_MODEL), F32),
            jax.ShapeDtypeStruct((n_seq, seq_len, D_MODEL), F32),
        ),
        grid=(n_seq, N_PAIRS, nb),
        in_specs=[
            pl.BlockSpec((None, MOBA_BLOCK, PAIR_W), lambda b, hp, qi: (b, qi, qb + hp)),
            seq_spec(kb),
            seq_spec(vb),
            tab_spec,
            tab_spec,
        ],
        out_specs=(
            pl.BlockSpec((None, MOBA_BLOCK, PAIR_W), lambda b, hp, qi: (b, qi, hp)),
            seq_spec(0),
            seq_spec(0),
        ),
        scratch_shapes=[
            pltpu.VMEM((nb, MOBA_BLOCK, PAIR_W), BF16),
            pltpu.VMEM((nb, 2, VT_ROWS, MOBA_BLOCK), BF16),
            pltpu.VMEM((nb, PAIR_W), F32),
            pltpu.VMEM((2, nb, SUBLANES, MOBA_BLOCK), F32),
        ],
        compiler_params=_cparams("parallel", "parallel", "arbitrary"),
        name="moba_prompt",
    )(proj3, proj3, proj3, cos_t, sin_t)


def _hi_lo(x):
    hi = x.astype(BF16)
    return hi, (x - hi.astype(F32)).astype(BF16)


def _mm3(dot_fn, a, b):
    ah, al = _hi_lo(a)
    bh, bl = _hi_lo(b)
    return dot_fn(ah, bh) + dot_fn(ah, bl) + dot_fn(al, bh)


def _mm1(dot_fn, a, b):
    return dot_fn(a.astype(BF16), b.astype(BF16))


_INV_MM = _mm1


def _head_ones3():
    r = lax.broadcasted_iota(jnp.int32, (3 * PAIR_W, PAIR_W), 0)
    c = lax.broadcasted_iota(jnp.int32, (3 * PAIR_W, PAIR_W), 1)
    return jnp.where((r % PAIR_W) // HEAD_DIM == c // HEAD_DIM, 1.0, 0.0).astype(BF16)


def _head_sums(x, ones3):
    hi = x.astype(BF16)
    r1 = x - hi.astype(F32)
    mid = r1.astype(BF16)
    lo = (r1 - mid.astype(F32)).astype(BF16)
    return _dot(jnp.concatenate([hi, mid, lo], axis=1), ones3)


LORA_W = DECAY_LORA + AAA_LORA


def _split3(x):
    hi = x.astype(BF16)
    r1 = x - hi.astype(F32)
    mid = r1.astype(BF16)
    return hi, mid, (r1 - mid.astype(F32)).astype(BF16)


def _chunk_decay_sums(lw, chunk, n_dec):
    tm = lw.shape[0]
    ci = lax.broadcasted_iota(jnp.int32, (tm, tm), 0)
    cj = lax.broadcasted_iota(jnp.int32, (tm, tm), 1)
    same = ci // chunk == cj // chunk
    di = lax.broadcasted_iota(jnp.int32, (n_dec, tm), 0)
    dj = lax.broadcasted_iota(jnp.int32, (n_dec, tm), 1)
    sel = jnp.concatenate([
        jnp.where(same & (ci >= cj), 1.0, 0.0),
        jnp.where(same, 1.0, 0.0),
        jnp.where(di == dj // chunk, 1.0, 0.0)], axis=0).astype(BF16)
    big = _dot(sel, jnp.concatenate(_split3(lw), axis=1))
    big = big[:, :D_MODEL] + big[:, D_MODEL:2 * D_MODEL] + big[:, 2 * D_MODEL:]
    return big[:tm], big[tm:2 * tm], big[2 * tm:]


def _rwkv_prep_kernel(p_ref, above_ref, ovr_ref, mu_ref, w0_ref, w2_ref, a0_ref, a2_ref, g2_ref, kk_ref, ka_ref,
                      rk_ref, kap_o, r_o, kt_o, bt_o, kh_o, bh_o, v_o, dec_o, g_o, bonus_o,
                      *, period, chunk, tiles_per_seq):
    p = p_ref[...]
    tm = p.shape[0]
    row = lax.broadcasted_iota(jnp.int32, (tm, 1), 0)
    if tiles_per_seq is None:
        shifted_in = ovr_ref[...]
    else:
        first = pl.program_id(0) % tiles_per_seq == 0
        shifted_in = jnp.where(first, ovr_ref[...], above_ref[SUBLANES - 1:SUBLANES, :])
    prev = jnp.where(row % period == 0, shifted_in, pltpu.roll(p, 1, 0))
    xm = p + (prev - p) * mu_ref[...]
    r = xm[:, :D_MODEL]
    k = xm[:, D_MODEL:2 * D_MODEL]
    v = xm[:, 2 * D_MODEL:3 * D_MODEL]
    lora = xm[:, 3 * D_MODEL:3 * D_MODEL + LORA_W]
    xg = xm[:, 3 * D_MODEL + LORA_W:]
    z = w0_ref[...] + jnp.dot(jnp.tanh(lora), w2_ref[...], precision=lax.Precision.HIGHEST,
                              preferred_element_type=F32)
    w_log = -jax.nn.softplus(-z) - 0.5
    a = jax.nn.sigmoid(a0_ref[...] + _dot(lora.astype(BF16), a2_ref[...]))
    g_o[...] = _dot(jax.nn.sigmoid(xg).astype(BF16), g2_ref[...])
    kk = k * kk_ref[...]
    k2 = k * (1.0 + (a - 1.0) * ka_ref[...])
    rk = r * k2 * rk_ref[...]
    lw = -jnp.exp(w_log)
    cum, tot, dec = _chunk_decay_sums(lw, chunk, dec_o.shape[0])
    dec_o[...] = jnp.exp(dec)
    g_in = jnp.exp(cum)
    g_ex = jnp.exp(cum - lw)
    g_inv = jnp.exp(-cum)
    g_end = jnp.exp(tot - cum)
    ones3 = _head_ones3()
    for pr in range(N_PAIRS):
        sl = slice(pr * PAIR_W, (pr + 1) * PAIR_W)
        kks = kk[:, sl]
        kap = kks / jnp.maximum(jnp.sqrt(_head_sums(kks * kks, ones3)), 1e-12)
        b = kap * a[:, sl]
        kap_o[pr] = (kap * g_ex[:, sl]).astype(BF16)
        r_o[pr] = r[:, sl] * g_in[:, sl]
        kt_o[pr] = (k2[:, sl] * g_inv[:, sl]).astype(BF16)
        bt_o[pr] = (b * g_inv[:, sl]).astype(BF16)
        kh_o[pr] = (k2[:, sl] * g_end[:, sl]).astype(BF16)
        bh_o[pr] = (b * g_end[:, sl]).astype(BF16)
        v_o[pr] = v[:, sl].astype(BF16)
        bonus_o[:, sl] = _head_sums(rk[:, sl], ones3) * v[:, sl]


def _rwkv_prep(p_rwkv, ovr, rw, *, tm, period, chunk):
    rows = p_rwkv.shape[0]
    n_dec = max(8, tm // chunk)
    if ovr.shape[1] == 1:
        tiles_per_seq = rows // ovr.shape[0] // tm
        assert period == tm
        ovr_spec = pl.BlockSpec((None, 1, SHIFT_W), lambda i: (i // tiles_per_seq, 0, 0))
    else:
        tiles_per_seq = None
        ovr_spec = pl.BlockSpec((None, tm, SHIFT_W), lambda i: (0, i, 0))
    above_spec = pl.BlockSpec((SUBLANES, SHIFT_W), lambda i: (jnp.maximum(i * (tm // SUBLANES) - 1, 0), 0))
    full = lambda a: pl.BlockSpec(a.shape, lambda i: (0,) * a.ndim)
    pm = lambda dt: jax.ShapeDtypeStruct((N_PAIRS, rows, PAIR_W), dt)
    rm = jax.ShapeDtypeStruct((rows, D_MODEL), F32)
    pm_spec = pl.BlockSpec((N_PAIRS, tm, PAIR_W), lambda i: (0, i, 0))
    rm_spec = pl.BlockSpec((tm, D_MODEL), lambda i: (i, 0))
    consts = (rw["mu"], rw["w0"], rw["w2"], rw["a0"], rw["a2"], rw["g2"], rw["k_k"], rw["k_a"], rw["r_k"])
    outs = pl.pallas_call(
        functools.partial(_rwkv_prep_kernel, period=period, chunk=chunk, tiles_per_seq=tiles_per_seq),
        out_shape=(pm(BF16), pm(F32)) + (pm(BF16),) * 5
        + (jax.ShapeDtypeStruct((rows // tm, n_dec, D_MODEL), F32), rm, rm),
        grid=(rows // tm,),
        in_specs=[pl.BlockSpec((tm, SHIFT_W), lambda i: (i, 0)), above_spec, ovr_spec]
        + [full(a) for a in consts],
        out_specs=(pm_spec,) * 7 + (pl.BlockSpec((None, n_dec, D_MODEL), lambda i: (i, 0, 0)), rm_spec, rm_spec),
        compiler_params=_cparams("parallel"),
        name="rwkv_prep",
    )(p_rwkv, p_rwkv, ovr, *consts)
    dec = outs[7][:, :tm // chunk].reshape(rows // chunk, 1, D_MODEL)
    return outs[:7], dec, outs[8], outs[9]


def _rwkv_weights(mu_shift, w0, w2_decay, a0, a2, g2, k_k, k_a, r_k):
    zeros = jnp.zeros((DECAY_LORA, D_MODEL), F32)
    return dict(
        mu=mu_shift.reshape(1, SHIFT_W), w0=w0.reshape(1, D_MODEL), a0=a0.reshape(1, D_MODEL),
        w2=jnp.concatenate([w2_decay, zeros], axis=0),
        a2=jnp.concatenate([zeros, a2], axis=0).astype(BF16),
        g2=g2.astype(BF16), k_k=k_k.reshape(1, D_MODEL), k_a=k_a.reshape(1, D_MODEL),
        r_k=r_k.reshape(1, D_MODEL))


INV_BASE = 8


def _unit_lower_inverses(l_ms, ti, tj):
    cl = l_ms[0].shape[0]
    base = min(INV_BASE, cl)
    eye = jnp.where(ti == tj, 1.0, 0.0)
    diag_blk = ti // base == tj // base
    l_bs = [jnp.where(diag_blk, l_m, 0.0) for l_m in l_ms]
    t_ms = [eye - l_b for l_b in l_bs]
    for _ in range(base.bit_length() - 2):
        l_bs = [_INV_MM(_dot, l_b, l_b) for l_b in l_bs]
        t_ms = [t_m + _INV_MM(_dot, t_m, l_b) for t_m, l_b in zip(t_ms, l_bs)]
    size = base
    while size < cl:
        lower_left = (ti // (2 * size) == tj // (2 * size)) & (ti // size != tj // size)
        tmp = [_INV_MM(_dot, t_m, jnp.where(lower_left, l_m, 0.0)) for t_m, l_m in zip(t_ms, l_ms)]
        t_ms = [t_m - _INV_MM(_dot, x, t_m) for t_m, x in zip(t_ms, tmp)]
        size *= 2
    return t_ms


def _rwkv_chunk_kernel(kap_ref, r_ref, kt_ref, bt_ref, kh_ref, bh_ref, v_ref, dec_ref, s0_ref, o_ref, so_ref,
                       st_scr):
    c = pl.program_id(1)
    n_c = pl.num_programs(1)
    cl = r_ref.shape[1]
    hd = HEAD_DIM

    @pl.when(c == 0)
    def _():
        z = jnp.zeros((hd, hd), F32)
        for pr in range(N_PAIRS):
            top = jnp.concatenate([s0_ref[2 * pr], z], axis=1)
            bot = jnp.concatenate([z, s0_ref[2 * pr + 1]], axis=1)
            st_scr[pr] = jnp.concatenate([top, bot], axis=0)

    ti = lax.broadcasted_iota(jnp.int32, (cl, cl), 0)
    tj = lax.broadcasted_iota(jnp.int32, (cl, cl), 1)
    tril = ti >= tj
    stril = ti > tj
    head1 = lax.broadcasted_iota(jnp.int32, (cl, PAIR_W), 1) >= hd
    sr = lax.broadcasted_iota(jnp.int32, (PAIR_W, PAIR_W), 0)
    sc = lax.broadcasted_iota(jnp.int32, (PAIR_W, PAIR_W), 1)
    same_head = (sr // hd) == (sc // hd)

    pairs = range(N_PAIRS)
    hs = [(pr, h) for pr in pairs for h in range(2)]
    bf = lambda xs: [x.astype(BF16) for x in xs]
    r_t = [r_ref[pr] for pr in pairs]
    kap_b = [kap_ref[pr] for pr in pairs]
    kt_b = [kt_ref[pr] for pr in pairs]
    bt_b = [bt_ref[pr] for pr in pairs]
    v_b = [v_ref[pr] for pr in pairs]
    masks = (jnp.logical_not(head1), head1)
    r_h = [jnp.where(masks[h], r_t[pr], 0.0) for pr, h in hs]
    r_hb = bf(r_h)
    kap_hb = [jnp.where(masks[h], kap_b[pr], jnp.zeros_like(kap_b[pr])) for pr, h in hs]
    rows4 = [jnp.concatenate([kap_hb[2 * pr], r_hb[2 * pr], kap_hb[2 * pr + 1], r_hb[2 * pr + 1]], axis=0)
             for pr in pairs]
    vs_b = [_dot_nt(rows4[pr], bt_b[pr]) for pr in pairs]
    vs_k = [_dot_nt(rows4[pr], kt_b[pr]) for pr in pairs]
    blk4 = lambda x, i, h: x[(2 * h + i) * cl:(2 * h + i + 1) * cl]
    l_m = [jnp.where(stril, blk4(vs_b[pr], 0, h), 0.0) for pr, h in hs]
    a_kk = [jnp.where(stril, blk4(vs_k[pr], 0, h), 0.0) for pr, h in hs]
    a_rk = [jnp.where(tril, blk4(vs_k[pr], 1, h), 0.0) for pr, h in hs]
    a_rb_b = bf([jnp.where(tril, blk4(vs_b[pr], 1, h), 0.0) for pr, h in hs])
    a_v = [_dot(jnp.concatenate(bf([a_kk[2 * pr], a_rk[2 * pr], a_kk[2 * pr + 1], a_rk[2 * pr + 1]]), axis=0),
                v_b[pr]) for pr in pairs]
    av = [blk4(a_v[pr], 0, h) for pr, h in hs]
    oi_h = [blk4(a_v[pr], 1, h) for pr, h in hs]
    t_b = bf(_unit_lower_inverses(l_m, ti, tj))
    wu = [_dot(t, jnp.concatenate([k, x], axis=1)) for t, k, x in zip(t_b, kap_hb, bf(av))]
    w_h = [x[:, :PAIR_W] for x in wu]
    u_h = [x[:, PAIR_W:] for x in wu]
    corr = [_dot(a, x.astype(BF16)) for a, x in zip(a_rb_b, wu)]
    rp_h = [r - x[:, :PAIR_W] for r, x in zip(r_h, corr)]
    oi_h = [o - x[:, PAIR_W:] for o, x in zip(oi_h, corr)]
    w_m = bf([w_h[2 * pr] + w_h[2 * pr + 1] for pr in pairs])
    u_m = bf([jnp.where(head1, u_h[2 * pr + 1], u_h[2 * pr]) for pr in pairs])
    rp = [rp_h[2 * pr] + rp_h[2 * pr + 1] for pr in pairs]
    oi = [jnp.where(head1, oi_h[2 * pr + 1], oi_h[2 * pr]) for pr in pairs]
    bh_b = [bh_ref[pr] for pr in pairs]
    kh_b = [kh_ref[pr] for pr in pairs]
    wtb = [jnp.where(same_head, _dot_tn(w_m[pr], bh_b[pr]), 0.0) for pr in pairs]
    n_t = [jnp.where(same_head, _dot_tn(v_b[pr], kh_b[pr]) - _dot_tn(u_m[pr], bh_b[pr]), 0.0) for pr in pairs]
    s = [st_scr[pr] for pr in pairs]
    o_new = [_mm3(_dot_nt, rp[pr], s[pr]) + oi[pr] for pr in pairs]
    s_new = [s[pr] * dec_ref[:, pr * PAIR_W:(pr + 1) * PAIR_W] - _mm3(_dot, s[pr], wtb[pr]) + n_t[pr]
             for pr in pairs]
    for pr in pairs:
        o_ref[pr] = o_new[pr]
        st_scr[pr] = s_new[pr]

    @pl.when(c == n_c - 1)
    def _():
        for pr in range(N_PAIRS):
            s = st_scr[pr]
            so_ref[2 * pr] = s[:hd, :hd]
            so_ref[2 * pr + 1] = s[hd:, hd:]


def _rwkv_chunks(pm_inputs, dec, s0, *, n_seq, chunk):
    rows = pm_inputs[0].shape[1]
    n_c = rows // n_seq // chunk
    pm_spec = pl.BlockSpec((N_PAIRS, chunk, PAIR_W), lambda s, c: (0, s * n_c + c, 0))
    st_spec = pl.BlockSpec((None, N_HEADS, HEAD_DIM, HEAD_DIM), lambda s, c: (s, 0, 0, 0))
    return pl.pallas_call(
        _rwkv_chunk_kernel,
        out_shape=(jax.ShapeDtypeStruct((N_PAIRS, rows, PAIR_W), F32),
                   jax.ShapeDtypeStruct((n_seq, N_HEADS, HEAD_DIM, HEAD_DIM), F32)),
        grid=(n_seq, n_c),
        in_specs=[pm_spec] * 7 + [pl.BlockSpec((None, 1, D_MODEL), lambda s, c: (s * n_c + c, 0, 0)), st_spec],
        out_specs=(pm_spec, st_spec),
        scratch_shapes=[pltpu.VMEM((N_PAIRS, PAIR_W, PAIR_W), F32)],
        compiler_params=_cparams("parallel", "arbitrary"),
        name="rwkv_chunks",
    )(*pm_inputs, dec, s0)


def _mixer_out_kernel(x_ref, gt_ref, npost_ref, o_ref, bonus_ref, g_ref, attn_ref, ga_ref, gb_ref,
                      lnw_ref, lnb_ref, wor_ref, wom_ref, wout_ref, y_ref):
    ones3 = _head_ones3()
    parts = []
    for pr in range(N_PAIRS):
        o = o_ref[pr]
        mu = _head_sums(o, ones3) * (1.0 / HEAD_DIM)
        d = o - mu
        var = _head_sums(d * d, ones3) * (1.0 / HEAD_DIM)
        parts.append(d * lax.rsqrt(var + LNX_EPS))
    on = jnp.concatenate(parts, axis=1) * lnw_ref[...] + lnb_ref[...]
    ya = _dot(((on + bonus_ref[...]) * g_ref[...]).astype(BF16), wor_ref[...])
    yb = _dot(attn_ref[...], wom_ref[...])
    y = jax.nn.sigmoid(ga_ref[...]) * ya + jax.nn.sigmoid(gb_ref[...]) * yb
    y = _dot(y.astype(BF16), wout_ref[...])
    y_ref[...] = x_ref[...] + gt_ref[...] * _rms(y, npost_ref[...])


def _mixer_out(x, mod, which, n_post, o_pm, bonus, g, attn, gates, lnx_w, lnx_b, w_o_rwkv, w_o_moba, w_out,
               *, tm, tiles_per_group):
    rows = x.shape[0]
    row_spec = pl.BlockSpec((tm, D_MODEL), lambda i: (i, 0))
    w_spec = pl.BlockSpec((D_MODEL, D_MODEL), lambda i: (0, 0))
    return pl.pallas_call(
        _mixer_out_kernel,
        out_shape=jax.ShapeDtypeStruct((rows, D_MODEL), F32),
        grid=(rows // tm,),
        in_specs=[
            row_spec,
            _mod_spec(mod, which, tm, tiles_per_group),
            _vec_spec(),
            pl.BlockSpec((N_PAIRS, tm, PAIR_W), lambda i: (0, i, 0)),
            row_spec, row_spec, row_spec,
            pl.BlockSpec((tm, D_MODEL), lambda i: (i, 0)),
            pl.BlockSpec((tm, D_MODEL), lambda i: (i, 1)),
            _vec_spec(), _vec_spec(), w_spec, w_spec, w_spec,
        ],
        out_specs=row_spec,
        compiler_params=_cparams("parallel"),
        name="mixer_out",
    )(x, mod, n_post, o_pm, bonus, g, attn, gates, gates, lnx_w, lnx_b, w_o_rwkv, w_o_moba, w_out)


def _rope_rows_kernel(qkv_ref, cos_ref, sin_ref, q_ref, k_ref):
    cos = cos_ref[...]
    sin = sin_ref[...]
    for pr in range(N_PAIRS):
        sl = slice(pr * PAIR_W, (pr + 1) * PAIR_W)
        q_ref[:, sl] = _rope_pair(qkv_ref[:, sl], cos, sin)
        k_ref[:, sl] = _rope_pair(qkv_ref[:, D_MODEL + pr * PAIR_W:D_MODEL + (pr + 1) * PAIR_W], cos, sin)


def _rope_rows(qkv, cos_t, sin_t):
    rows = qkv.shape[0]
    out = jax.ShapeDtypeStruct((rows, D_MODEL), F32)
    return pl.pallas_call(_rope_rows_kernel, out_shape=(out, out), name="rope_rows")(qkv, cos_t, sin_t)


TOK_SLOTS = 8


PAGES_PER_STEP = 8
PAGES_PER_BLOCK = MOBA_BLOCK // PAGE_SIZE


def _moba_sample_kernel(pt_ref, q_ref, kn_ref, vn_ref, *refs, n_seq, n_pages, n_tok):
    pps = PAGES_PER_STEP
    kp_refs, vp_refs = refs[:pps], refs[pps:2 * pps]
    o_ref, q2_scr, s_scr, pn_scr, acc_scr, l_scr = refs[2 * pps:]
    b = pl.program_id(0)
    n = pl.program_id(1)
    n_steps = pl.num_programs(1)
    rows = N_HEADS * TOK_SLOTS
    n_blk = n_pages // PAGES_PER_BLOCK
    blk_per_step = pps // PAGES_PER_BLOCK
    scale = HEAD_DIM ** -0.5
    row_head = lax.broadcasted_iota(jnp.int32, (rows, D_MODEL), 0) // TOK_SLOTS
    lane_head = lax.broadcasted_iota(jnp.int32, (rows, D_MODEL), 1) // HEAD_DIM
    slot_k = b % 2
    slot_v = (b + 1) % 2

    @pl.when((n == 0) & (b < n_seq))
    def _():
        q_bd = jnp.where(row_head == lane_head, jnp.tile(q_ref[...], (N_HEADS, 1)), 0.0)
        hi, lo = _hi_lo(q_bd)
        q2_scr[:rows] = hi
        q2_scr[rows:] = lo

    @pl.when(b < n_seq)
    def _():
        kh, kl = _hi_lo(jnp.concatenate([r[...] for r in kp_refs], axis=1))
        top = _dot(q2_scr[...], kh)
        s = top[:rows] + top[rows:] + _dot(q2_scr[:rows], kl)
        for i in range(blk_per_step):
            s_scr[slot_k, n * blk_per_step + i] = s[:, i * MOBA_BLOCK:(i + 1) * MOBA_BLOCK]

    @pl.when((n == n_steps - 1) & (b < n_seq))
    def _():
        cols = [jnp.sum(s_scr[slot_k, i], axis=1, keepdims=True) for i in range(n_blk)]
        gate = jnp.concatenate(cols, axis=1)
        col = lax.broadcasted_iota(jnp.int32, gate.shape, 1)
        cnt = jnp.zeros(gate.shape, jnp.int32)
        for n in range(n_blk):
            gn = gate[:, n:n + 1]
            cnt = cnt + jnp.where((gn > gate) | ((gn == gate) & (n < col)), 1, 0)
        bias = jnp.where(cnt < MOBA_TOPK, 0.0, NEG_INF)
        knh, knl = _hi_lo(kn_ref[...])
        qh = q2_scr[:rows]
        s_new = (_dot_nt(qh, knh) + _dot_nt(q2_scr[rows:], knh) + _dot_nt(qh, knl)) * scale
        t_q = lax.broadcasted_iota(jnp.int32, s_new.shape, 0) % TOK_SLOTS
        t_k = lax.broadcasted_iota(jnp.int32, s_new.shape, 1)
        s_new = jnp.where((t_k <= t_q) & (t_k < n_tok), s_new, NEG_INF)
        logits = [s_scr[slot_k, i] * scale + bias[:, i:i + 1] for i in range(n_blk)]
        mx = functools.reduce(jnp.maximum, logits)
        m = jnp.maximum(jnp.max(mx, axis=1, keepdims=True), jnp.max(s_new, axis=1, keepdims=True))
        p_new = jnp.exp(s_new - m)
        ps = [jnp.exp(x - m) for x in logits]
        for i in range(n_blk):
            s_scr[slot_k, i] = ps[i]
        tot = functools.reduce(jnp.add, ps)
        l_scr[slot_k] = jnp.sum(tot, axis=1, keepdims=True) + jnp.sum(p_new, axis=1, keepdims=True)
        pn_scr[slot_k] = p_new

    @pl.when(b >= 1)
    def _():
        @pl.when(n == 0)
        def _():
            acc_scr[...] = _dot(pn_scr[slot_v], vn_ref[...])

        for i in range(blk_per_step):
            vb = jnp.concatenate([r[...] for r in vp_refs[i * PAGES_PER_BLOCK:(i + 1) * PAGES_PER_BLOCK]], axis=1)
            acc_scr[...] += _dot_nt(s_scr[slot_v, n * blk_per_step + i].astype(BF16), vb.astype(BF16))

        @pl.when(n == n_steps - 1)
        def _():
            acc = jnp.where(row_head == lane_head, acc_scr[...] / l_scr[slot_v], 0.0)
            out = acc[:TOK_SLOTS]
            for h in range(1, N_HEADS):
                out = out + acc[h * TOK_SLOTS:(h + 1) * TOK_SLOTS]
            o_ref[...] = out


def _moba_sample(page_table, q_rot, k_new, v_new, cache_kt, cache_vt, *, n_tok):
    n_seq, n_pages = page_table.shape
    pps = PAGES_PER_STEP
    assert n_pages % pps == 0 and pps % PAGES_PER_BLOCK == 0 and n_tok <= TOK_SLOTS
    rows = N_HEADS * TOK_SLOTS
    n_blk = n_pages // PAGES_PER_BLOCK
    cur = lambda b: jnp.minimum(b, n_seq - 1)
    prev = lambda b: jnp.maximum(b - 1, 0)
    cur_spec = pl.BlockSpec((None, TOK_SLOTS, D_MODEL), lambda b, n, pt: (cur(b), 0, 0))
    prev_spec = pl.BlockSpec((None, TOK_SLOTS, D_MODEL), lambda b, n, pt: (prev(b), 0, 0))

    def page_spec(seq_of, i):
        return pl.BlockSpec((None, D_MODEL, PAGE_SIZE),
                            lambda b, n, pt: (pt[seq_of(b) * n_pages + n * pps + i], 0, 0))

    return pl.pallas_call(
        functools.partial(_moba_sample_kernel, n_seq=n_seq, n_pages=n_pages, n_tok=n_tok),
        out_shape=jax.ShapeDtypeStruct((n_seq, TOK_SLOTS, D_MODEL), F32),
        grid_spec=pltpu.PrefetchScalarGridSpec(
            num_scalar_prefetch=1,
            grid=(n_seq + 1, n_pages // pps),
            in_specs=[cur_spec, cur_spec, prev_spec]
            + [page_spec(cur, i) for i in range(pps)] + [page_spec(prev, i) for i in range(pps)],
            out_specs=prev_spec,
            scratch_shapes=[
                pltpu.VMEM((2 * rows, D_MODEL), BF16),
                pltpu.VMEM((2, n_blk, rows, MOBA_BLOCK), F32),
                pltpu.VMEM((2, rows, TOK_SLOTS), F32),
                pltpu.VMEM((rows, D_MODEL), F32),
                pltpu.VMEM((2, rows, 1), F32),
            ],
        ),
        compiler_params=_cparams("arbitrary", "arbitrary"),
        name="moba_sample",
    )(page_table.reshape(-1), q_rot, k_new, v_new, *([cache_kt] * pps), *([cache_vt] * pps))


FFN_TM = 512
INPROJ_TM = 512
RWKV_TM = 256
PROMPT_CHUNK = 64
SAMPLE_CHUNK = 16


def _layer(x, mod, lw, *, group_rows, shift_ovr, shift_period, prep_chunk, attend, rwkv_scan):
    tm_ffn, tm_in = min(FFN_TM, group_rows), min(INPROJ_TM, group_rows)
    x1 = _ffn(x, mod, 0, lw["n1_pre"], lw["n1_post"], lw["w1_gu"], lw["w1_down"], tm=tm_ffn,
              tiles_per_group=group_rows // tm_ffn)
    p_rwkv, qkv, gates = _inproj(x1, mod, 3, lw["n2_pre"], lw["w_in"], tm=tm_in,
                                 tiles_per_group=group_rows // tm_in)
    attn, extras = attend(qkv)
    pm, dec, g, bonus = _rwkv_prep(p_rwkv, shift_ovr, lw["rwkv"], tm=RWKV_TM, period=shift_period,
                                   chunk=prep_chunk)
    o_pm, state = rwkv_scan(pm, dec)
    x2 = _mixer_out(x1, mod, 5, lw["n2_post"], o_pm, bonus, g, attn, gates, lw["lnx_w"], lw["lnx_b"],
                    lw["w_o_rwkv"], lw["w_o_moba"], lw["w_out"], tm=RWKV_TM,
                    tiles_per_group=group_rows // RWKV_TM)
    y = _ffn(x2, mod, 6, lw["n3_pre"], lw["n3_post"], lw["w3_gu"], lw["w3_down"], tm=tm_ffn,
             tiles_per_group=group_rows // tm_ffn)
    return y, p_rwkv, qkv, state, extras


def kernel(x_prompt, x_sample, cache_k, cache_v, state_rwkv, state_shift, page_table, c_prompt, c_sample, w_ada, b_ada, n1_pre, n1_post, w1_gu, w1_down, n2_pre, n2_post, w_in, mu_shift, w0, w2_decay, a0, a2, g2, k_k, k_a, r_k, lnx_w, lnx_b, w_o_rwkv, w_o_moba, w_out, n3_pre, n3_post, w3_gu, w3_down):
    assert w_ada.shape[0] == 1, "single layer"
    bp, seq, _ = x_prompt.shape
    bs, ts, _ = x_sample.shape
    n_pool = cache_k.shape[1]
    n_pages = page_table.shape[1]
    rows_p, rows_s = bp * seq, bs * ts
    assert seq % INPROJ_TM == 0 and seq % MOBA_BLOCK == 0 and rows_s % FFN_TM == 0 and ts <= SAMPLE_CHUNK

    lw = dict(
        n1_pre=n1_pre, n1_post=n1_post, n2_pre=n2_pre, n2_post=n2_post, n3_pre=n3_pre, n3_post=n3_post,
        w1_gu=w1_gu[0].astype(BF16), w1_down=w1_down[0].astype(BF16),
        w3_gu=w3_gu[0].astype(BF16), w3_down=w3_down[0].astype(BF16),
        w_in=_split_w_in(w_in[0]),
        rwkv=_rwkv_weights(mu_shift[0], w0[0], w2_decay[0], a0[0], a2[0], g2[0], k_k[0], k_a[0], r_k[0]),
        lnx_w=lnx_w, lnx_b=lnx_b,
        w_o_rwkv=w_o_rwkv[0].astype(BF16), w_o_moba=w_o_moba[0].astype(BF16), w_out=w_out[0].astype(BF16),
    )

    n_c = bp + bs
    c_all = jnp.concatenate([c_prompt, c_sample, jnp.zeros((-n_c % 8, D_MODEL), F32)], axis=0)
    mod = _ada(c_all, w_ada[0], b_ada[0])
    mod_p = mod[:bp].reshape(bp, N_ADA, 1, D_MODEL).transpose(1, 0, 2, 3)
    mod_s = jnp.repeat(mod[bp:n_c].reshape(bs, N_ADA, D_MODEL), ts, axis=0).transpose(1, 0, 2)[:, None]

    cos_p, sin_p = _rope_tables(jnp.arange(seq, dtype=jnp.int32))

    def attend_p(qkv):
        attn, k_rot, v = _moba_prompt(qkv, cos_p, sin_p, bp, seq)
        return attn.reshape(rows_p, D_MODEL), (k_rot, v)

    def scan_p(pm, dec):
        s0 = jnp.zeros((bp, N_HEADS, HEAD_DIM, HEAD_DIM), F32)
        return _rwkv_chunks(pm, dec, s0, n_seq=bp, chunk=PROMPT_CHUNK)

    y_p, p_p, _, state_p, (k_p, v_p) = _layer(
        x_prompt.reshape(rows_p, D_MODEL), mod_p, lw, group_rows=seq,
        shift_ovr=jnp.zeros((bp, 1, SHIFT_W), F32), shift_period=RWKV_TM, prep_chunk=PROMPT_CHUNK,
        attend=attend_p, rwkv_scan=scan_p)

    pos_s = n_pages * PAGE_SIZE + jnp.arange(ts, dtype=jnp.int32)
    cos_s, sin_s = (jnp.tile(t, (bs, 1)) for t in _rope_tables(pos_s))
    pad_tok = lambda a: jnp.pad(a.reshape(bs, ts, D_MODEL), ((0, 0), (0, TOK_SLOTS - ts), (0, 0)))
    cache_kt = cache_k[0].transpose(0, 2, 3, 1).reshape(n_pool, D_MODEL, PAGE_SIZE)
    cache_vt = cache_v[0].transpose(0, 2, 3, 1).reshape(n_pool, D_MODEL, PAGE_SIZE)

    def attend_s(qkv):
        q_rot, k_rot = _rope_rows(qkv, cos_s, sin_s)
        v = qkv[:, 2 * D_MODEL:]
        attn = _moba_sample(page_table, pad_tok(q_rot), pad_tok(k_rot), pad_tok(v), cache_kt, cache_vt, n_tok=ts)
        return attn[:, :ts].reshape(rows_s, D_MODEL).astype(BF16), (k_rot, v)

    def scan_s(pm, dec):
        pad = lambda a: jnp.pad(a.reshape(N_PAIRS, bs, ts, PAIR_W),
                                ((0, 0), (0, 0), (0, SAMPLE_CHUNK - ts), (0, 0))).reshape(N_PAIRS, -1, PAIR_W)
        o_pm, state = _rwkv_chunks([pad(a) for a in pm], dec, state_rwkv[0], n_seq=bs, chunk=SAMPLE_CHUNK)
        o_pm = o_pm.reshape(N_PAIRS, bs, SAMPLE_CHUNK, PAIR_W)[:, :, :ts].reshape(N_PAIRS, rows_s, PAIR_W)
        return o_pm, state

    y_s, p_s, _, state_s, (k_s, v_s) = _layer(
        x_sample.reshape(rows_s, D_MODEL), mod_s, lw, group_rows=rows_s,
        shift_ovr=jnp.repeat(state_shift[0], ts, axis=0)[None], shift_period=ts, prep_chunk=ts,
        attend=attend_s, rwkv_scan=scan_s)

    heads = lambda a, b, t: a.reshape(1, b, t, N_HEADS, HEAD_DIM)
    return (
        y_p.reshape(bp, seq, D_MODEL),
        y_s.reshape(bs, ts, D_MODEL),
        heads(k_p, bp, seq), heads(v_p, bp, seq),
        state_p[None], p_p.reshape(bp, seq, SHIFT_W)[:, -1][None],
        heads(k_s, bs, ts), heads(v_s, bs, ts),
        state_s[None], p_s.reshape(bs, ts, SHIFT_W)[:, -1][None],
    )
```

```python
import functools

import jax
import jax.numpy as jnp
from jax import lax
from jax.experimental import pallas as pl
from jax.experimental.pallas import tpu as pltpu

D_MODEL = 1024
HEAD_DIM = 64
N_HEADS = D_MODEL // HEAD_DIM
PAIR_W = 2 * HEAD_DIM
N_PAIRS = N_HEADS // 2
DECAY_LORA = 64
AAA_LORA = 64
GATE_LORA = 128
SHIFT_W = 3 * D_MODEL + DECAY_LORA + AAA_LORA + GATE_LORA
IN_W = SHIFT_W + 3 * D_MODEL + 2 * D_MODEL
D_FF = 11 * D_MODEL // 4
N_ADA = 9
MACARON_W = 0.5
RMS_EPS = 1e-6
LNX_EPS = 64e-5
MOBA_BLOCK = 256
MOBA_TOPK = 3
PAGE_SIZE = 128
ROPE_THETA = 10000.0

SUBLANES = 8
VMEM_LIMIT = 56 * 1024 * 1024

BF16 = jnp.bfloat16
F32 = jnp.float32
NEG_INF = float("-inf")


def _cparams(*sem):
    return pltpu.CompilerParams(dimension_semantics=sem, vmem_limit_bytes=VMEM_LIMIT)


def _rms(x, g):
    return x * lax.rsqrt(jnp.mean(x * x, axis=-1, keepdims=True) + RMS_EPS) * g


def _dot(a, b):
    return jnp.dot(a, b, preferred_element_type=F32)


def _dot_nt(a, b):
    return lax.dot_general(a, b, (((1,), (1,)), ((), ())), preferred_element_type=F32)


def _dot_tn(a, b):
    return lax.dot_general(a, b, (((0,), (0,)), ((), ())), preferred_element_type=F32)


ADA_TN = 1536


def _ada_kernel(c_ref, w_ref, b_ref, o_ref):
    c = c_ref[...]
    a = (c * jax.nn.sigmoid(c)).astype(BF16)
    o_ref[...] = _dot(a, w_ref[...].astype(BF16)) + b_ref[...]


def _ada(c_all, w_ada, b_ada):
    rows = c_all.shape[0]
    n_out = w_ada.shape[1]
    return pl.pallas_call(
        _ada_kernel,
        out_shape=jax.ShapeDtypeStruct((rows, n_out), F32),
        grid=(n_out // ADA_TN,),
        in_specs=[
            pl.BlockSpec((rows, D_MODEL), lambda j: (0, 0)),
            pl.BlockSpec((D_MODEL, ADA_TN), lambda j: (0, j)),
            pl.BlockSpec((1, ADA_TN), lambda j: (0, j)),
        ],
        out_specs=pl.BlockSpec((rows, ADA_TN), lambda j: (0, j)),
        compiler_params=_cparams("arbitrary"),
        name="ada_mod",
    )(c_all, w_ada, b_ada.reshape(1, n_out))


def _mod_spec(mod, which, tm, tiles_per_group):
    rows_in_group = mod.shape[2]
    if rows_in_group == 1:
        return pl.BlockSpec((None, None, 1, D_MODEL), lambda i, *_: (which, i // tiles_per_group, 0, 0))
    return pl.BlockSpec((None, None, tm, D_MODEL), lambda i, *_: (which, 0, i, 0))


def _vec_spec():
    return pl.BlockSpec((1, D_MODEL), lambda i, *_: (0, 0))


FFN_TF = D_FF // 2
FFN_SUB = 768


def _ffn_kernel(x_ref, sh_ref, sc_ref, gt_ref, npre_ref, npost_ref, wg_ref, wu_ref, wd_ref, o_ref, h_scr, acc_scr):
    j = pl.program_id(1)

    @pl.when(j == 0)
    def _():
        h = _rms(x_ref[...], npre_ref[...]) * (1.0 + sc_ref[...]) + sh_ref[...]
        h_scr[...] = h.astype(BF16)
        acc_scr[...] = jnp.zeros_like(acc_scr)

    h = h_scr[...]
    for lo in range(0, FFN_TF, FFN_SUB):
        cols = slice(lo, min(lo + FFN_SUB, FFN_TF))
        g = _dot(h, wg_ref[:, cols])
        u = _dot(h, wu_ref[:, cols])
        a = (g * jax.nn.sigmoid(g) * u).astype(BF16)
        acc_scr[...] += _dot(a, wd_ref[cols, :])

    @pl.when(j == pl.num_programs(1) - 1)
    def _():
        y = _rms(acc_scr[...], npost_ref[...])
        o_ref[...] = x_ref[...] + MACARON_W * gt_ref[...] * y


def _ffn(x, mod, which0, n_pre, n_post, wgu, wd, *, tm, tiles_per_group):
    rows = x.shape[0]
    nf = D_FF // FFN_TF
    row_spec = pl.BlockSpec((tm, D_MODEL), lambda i, j: (i, 0))
    return pl.pallas_call(
        _ffn_kernel,
        out_shape=jax.ShapeDtypeStruct((rows, D_MODEL), F32),
        grid=(rows // tm, nf),
        in_specs=[
            row_spec,
            _mod_spec(mod, which0, tm, tiles_per_group),
            _mod_spec(mod, which0 + 1, tm, tiles_per_group),
            _mod_spec(mod, which0 + 2, tm, tiles_per_group),
            _vec_spec(),
            _vec_spec(),
            pl.BlockSpec((D_MODEL, FFN_TF), lambda i, j: (0, j)),
            pl.BlockSpec((D_MODEL, FFN_TF), lambda i, j: (0, j + nf)),
            pl.BlockSpec((FFN_TF, D_MODEL), lambda i, j: (j, 0)),
        ],
        out_specs=row_spec,
        scratch_shapes=[pltpu.VMEM((tm, D_MODEL), BF16), pltpu.VMEM((tm, D_MODEL), F32)],
        compiler_params=_cparams("parallel", "arbitrary"),
        name="ffn",
    )(x, mod, mod, mod, n_pre, n_post, wgu, wgu, wd)


INPROJ_WIDTHS = (SHIFT_W, 3 * D_MODEL, 2 * D_MODEL)
INPROJ_SPLIT = 2


def _inproj_kernel(x_ref, sh_ref, sc_ref, npre_ref, w0_ref, w1_ref, w2_ref, o0_ref, o1_ref, o2_ref, h_scr):
    j = pl.program_id(1)

    @pl.when(j == 0)
    def _():
        h = _rms(x_ref[...], npre_ref[...]) * (1.0 + sc_ref[...]) + sh_ref[...]
        h_scr[...] = h.astype(BF16)

    for sec, (w_ref, o_ref) in enumerate(((w0_ref, o0_ref), (w1_ref, o1_ref), (w2_ref, o2_ref))):
        @pl.when(j // INPROJ_SPLIT == sec)
        def _(w_ref=w_ref, o_ref=o_ref):
            o_ref[...] = _dot(h_scr[...], w_ref[...])


def _inproj(x, mod, which0, n_pre, w_secs, *, tm, tiles_per_group):
    rows = x.shape[0]

    def sec_col(sec):
        return lambda i, j: jnp.clip(j - sec * INPROJ_SPLIT, 0, INPROJ_SPLIT - 1)

    w_specs, o_specs = [], []
    for sec, width in enumerate(INPROJ_WIDTHS):
        tn = width // INPROJ_SPLIT
        col = sec_col(sec)
        w_specs.append(pl.BlockSpec((D_MODEL, tn), lambda i, j, col=col: (0, col(i, j))))
        o_specs.append(pl.BlockSpec((tm, tn), lambda i, j, col=col: (i, col(i, j))))
    return pl.pallas_call(
        _inproj_kernel,
        out_shape=tuple(jax.ShapeDtypeStruct((rows, w), F32) for w in INPROJ_WIDTHS),
        grid=(rows // tm, len(INPROJ_WIDTHS) * INPROJ_SPLIT),
        in_specs=[
            pl.BlockSpec((tm, D_MODEL), lambda i, j: (i, 0)),
            _mod_spec(mod, which0, tm, tiles_per_group),
            _mod_spec(mod, which0 + 1, tm, tiles_per_group),
            _vec_spec(),
            *w_specs,
        ],
        out_specs=tuple(o_specs),
        scratch_shapes=[pltpu.VMEM((tm, D_MODEL), BF16)],
        compiler_params=_cparams("parallel", "arbitrary"),
        name="inproj",
    )(x, mod, mod, n_pre, *w_secs)


def _split_w_in(w_in):
    w = w_in.astype(BF16)
    c1 = SHIFT_W
    c2 = SHIFT_W + 3 * D_MODEL
    return w[:, :c1], w[:, c1:c2], w[:, c2:]


def _rope_tables(pos):
    half = HEAD_DIM // 2
    inv = ROPE_THETA ** (-jnp.arange(half, dtype=F32) / half)
    ang = pos.astype(F32)[:, None] * inv[None, :]
    cos, sin = jnp.cos(ang), jnp.sin(ang)
    return jnp.tile(cos, (1, 4)), jnp.tile(jnp.concatenate([-sin, sin], axis=1), (1, 2))


def _rope_pair(x, cos, sin_signed):
    lane = lax.broadcasted_iota(jnp.int32, x.shape, 1)
    first_half = (lane % HEAD_DIM) < HEAD_DIM // 2
    partner = jnp.where(first_half, pltpu.roll(x, PAIR_W - HEAD_DIM // 2, 1), pltpu.roll(x, HEAD_DIM // 2, 1))
    return x * cos + partner * sin_signed


LOG2_E = 1.4426950408889634
BLOCKS_PER_TRIP = 4
VT_ROWS = HEAD_DIM + 16


def _topk_bias(gate, n_past):
    nb = gate.shape[0]
    blk = lax.broadcasted_iota(jnp.int32, gate.shape, 0)
    cnt = jnp.zeros(gate.shape, jnp.int32)
    for jp in range(nb):
        gj = gate[jp:jp + 1, :]
        beats = (gj > gate) | ((gj == gate) & (jp < blk))
        cnt = cnt + jnp.where(beats & (jp < n_past), 1, 0)
    sel = (blk < n_past) & (cnt < MOBA_TOPK)
    return jnp.where(sel, 0.0, NEG_INF)


def _moba_prompt_kernel(q_ref, k_ref, v_ref, cos_ref, sin_ref, o_ref, ko_ref, vo_ref,
                        kbf_scr, vt_scr, kmean_scr, bias_scr):
    qi = pl.program_id(2)
    nb = kbf_scr.shape[0]
    blk = MOBA_BLOCK

    @pl.when(qi == 0)
    def _():
        def prep(j, carry):
            rows = pl.ds(pl.multiple_of(j * blk, blk), blk)
            kr = _rope_pair(k_ref[rows, :], cos_ref[rows, :], sin_ref[rows, :])
            ko_ref[rows, :] = kr
            kbf_scr[j] = kr.astype(BF16)
            kmean_scr[pl.ds(j, 1), :] = jnp.mean(kr, axis=0, keepdims=True)
            v = v_ref[rows, :]
            vo_ref[rows, :] = v
            vt = v.T.astype(BF16)
            for h in range(2):
                vt_scr[j, h, :HEAD_DIM, :] = vt[h * HEAD_DIM:(h + 1) * HEAD_DIM]
                vt_scr[j, h, HEAD_DIM:, :] = ones_rows
            return carry

        pad_row = lax.broadcasted_iota(jnp.int32, (VT_ROWS - HEAD_DIM, blk), 0)
        ones_rows = jnp.where(pad_row == 0, 1.0, 0.0).astype(BF16)
        lax.fori_loop(0, nb, prep, 0)

    rows_q = pl.ds(pl.multiple_of(qi * blk, blk), blk)
    q = _rope_pair(q_ref[...], cos_ref[rows_q, :], sin_ref[rows_q, :])
    lane = lax.broadcasted_iota(jnp.int32, q.shape, 1)
    kmean = kmean_scr[...]
    kidx = lax.broadcasted_iota(jnp.int32, (blk, blk), 0)
    qidx = lax.broadcasted_iota(jnp.int32, (blk, blk), 1)
    heads = range(2)
    hb = [(h, b) for h in heads for b in range(BLOCKS_PER_TRIP)]
    qh = [jnp.where(lane // HEAD_DIM == h, q, 0.0) for h in heads]
    km_lane = lax.broadcasted_iota(jnp.int32, kmean.shape, 1)
    kmean_by_head = jnp.concatenate([jnp.where(km_lane // HEAD_DIM == h, kmean, 0.0) for h in heads], axis=0)
    gate = lax.dot_general(kmean_by_head, q, (((1,), (1,)), ((), ())), precision=lax.Precision.HIGHEST,
                           preferred_element_type=F32)
    for h in heads:
        bias = _topk_bias(gate[h * nb:(h + 1) * nb], qi)
        for j in range(nb):
            bias_scr[h, j] = jnp.broadcast_to(bias[j:j + 1], (SUBLANES, blk))
    qs = [(qh[h] * (HEAD_DIM ** -0.5 * LOG2_E)).astype(BF16) for h in heads]

    def shifted_exp2(s, shift):
        x = s.reshape(blk // SUBLANES, SUBLANES, blk) + shift[None]
        return jnp.exp2(x).reshape(blk, blk).astype(BF16)

    k_own = kbf_scr[qi]
    s_own = [jnp.where(kidx <= qidx, _dot_nt(k_own, qs[h]), NEG_INF) for h in heads]
    m_own = [jnp.max(s_own[h], axis=0, keepdims=True) for h in heads]
    acc_own = [_dot(vt_scr[qi, h], jnp.exp2(s_own[h] - m_own[h]).astype(BF16)) for h in heads]

    def trip(u, carry):
        m, acc = carry
        js = [jnp.minimum(BLOCKS_PER_TRIP * u + b, nb - 1) for b in range(BLOCKS_PER_TRIP)]
        s = [_dot_nt(kbf_scr[js[b]], qs[h]) for h, b in hb]
        bias = [bias_scr[h, js[b]] for h, b in hb]
        bm = [jnp.max(x, axis=0, keepdims=True) + c[:1] for x, c in zip(s, bias)]
        m_new = [jnp.maximum(m[h], functools.reduce(jnp.maximum, bm[h * BLOCKS_PER_TRIP:(h + 1) * BLOCKS_PER_TRIP]))
                 for h in heads]
        alpha = [jnp.exp2(m[h] - m_new[h]) for h in heads]
        p = [shifted_exp2(x, c - m_new[h]) for x, c, (h, b) in zip(s, bias, hb)]
        pv = [_dot(vt_scr[js[b], h], x) for x, (h, b) in zip(p, hb)]
        acc = [alpha[h] * acc[h] + functools.reduce(jnp.add, pv[h * BLOCKS_PER_TRIP:(h + 1) * BLOCKS_PER_TRIP])
               for h in heads]
        return m_new, acc

    n_trips = (qi + BLOCKS_PER_TRIP - 1) // BLOCKS_PER_TRIP
    _, acc = lax.fori_loop(0, n_trips, trip, (m_own, acc_own))
    out = [acc[h][:HEAD_DIM] / acc[h][HEAD_DIM:HEAD_DIM + 1] for h in heads]
    o_ref[...] = jnp.concatenate(out, axis=0).T.astype(o_ref.dtype)


def _moba_prompt(qkv, cos_t, sin_t, n_seq, seq_len):
    proj3 = qkv.reshape(n_seq, seq_len, 3 * D_MODEL)
    nb = seq_len // MOBA_BLOCK
    qb, kb, vb = 0, N_PAIRS, 2 * N_PAIRS
    seq_spec = lambda col0: pl.BlockSpec((None, seq_len, PAIR_W), lambda b, hp, qi: (b, 0, col0 + hp))
    tab_spec = pl.BlockSpec((seq_len, PAIR_W), lambda b, hp, qi: (0, 0))
    return pl.pallas_call(
        _moba_prompt_kernel,
        out_shape=(
            jax.ShapeDtypeStruct((n_seq, seq_len, D_MODEL), BF16),
            jax.ShapeDtypeStruct((n_seq, seq_len, D_MODEL), F32),
            jax.ShapeDtypeStruct((n_seq, seq_len, D_MODEL), F32),
        ),
        grid=(n_seq, N_PAIRS, nb),
        in_specs=[
            pl.BlockSpec((None, MOBA_BLOCK, PAIR_W), lambda b, hp, qi: (b, qi, qb + hp)),
            seq_spec(kb),
            seq_spec(vb),
            tab_spec,
            tab_spec,
        ],
        out_specs=(
            pl.BlockSpec((None, MOBA_BLOCK, PAIR_W), lambda b, hp, qi: (b, qi, hp)),
            seq_spec(0),
            seq_spec(0),
        ),
        scratch_shapes=[
            pltpu.VMEM((nb, MOBA_BLOCK, PAIR_W), BF16),
            pltpu.VMEM((nb, 2, VT_ROWS, MOBA_BLOCK), BF16),
            pltpu.VMEM((nb, PAIR_W), F32),
            pltpu.VMEM((2, nb, SUBLANES, MOBA_BLOCK), F32),
        ],
        compiler_params=_cparams("parallel", "parallel", "arbitrary"),
        name="moba_prompt",
    )(proj3, proj3, proj3, cos_t, sin_t)


def _hi_lo(x):
    hi = x.astype(BF16)
    return hi, (x - hi.astype(F32)).astype(BF16)


def _mm3(dot_fn, a, b):
    ah, al = _hi_lo(a)
    bh, bl = _hi_lo(b)
    return dot_fn(ah, bh) + dot_fn(ah, bl) + dot_fn(al, bh)


def _mm1(dot_fn, a, b):
    return dot_fn(a.astype(BF16), b.astype(BF16))


_INV_MM = _mm1


def _head_ones3():
    r = lax.broadcasted_iota(jnp.int32, (3 * PAIR_W, PAIR_W), 0)
    c = lax.broadcasted_iota(jnp.int32, (3 * PAIR_W, PAIR_W), 1)
    return jnp.where((r % PAIR_W) // HEAD_DIM == c // HEAD_DIM, 1.0, 0.0).astype(BF16)


def _head_sums(x, ones3):
    hi = x.astype(BF16)
    r1 = x - hi.astype(F32)
    mid = r1.astype(BF16)
    lo = (r1 - mid.astype(F32)).astype(BF16)
    return _dot(jnp.concatenate([hi, mid, lo], axis=1), ones3)


LORA_W = DECAY_LORA + AAA_LORA


def _split3(x):
    hi = x.astype(BF16)
    r1 = x - hi.astype(F32)
    mid = r1.astype(BF16)
    return hi, mid, (r1 - mid.astype(F32)).astype(BF16)


def _chunk_decay_sums(lw, chunk, n_dec):
    tm = lw.shape[0]
    ci = lax.broadcasted_iota(jnp.int32, (tm, tm), 0)
    cj = lax.broadcasted_iota(jnp.int32, (tm, tm), 1)
    same = ci // chunk == cj // chunk
    di = lax.broadcasted_iota(jnp.int32, (n_dec, tm), 0)
    dj = lax.broadcasted_iota(jnp.int32, (n_dec, tm), 1)
    sel = jnp.concatenate([
        jnp.where(same & (ci >= cj), 1.0, 0.0),
        jnp.where(same, 1.0, 0.0),
        jnp.where(di == dj // chunk, 1.0, 0.0)], axis=0).astype(BF16)
    big = _dot(sel, jnp.concatenate(_split3(lw), axis=1))
    big = big[:, :D_MODEL] + big[:, D_MODEL:2 * D_MODEL] + big[:, 2 * D_MODEL:]
    return big[:tm], big[tm:2 * tm], big[2 * tm:]


def _rwkv_prep_kernel(p_ref, above_ref, ovr_ref, mu_ref, w0_ref, w2_ref, a0_ref, a2_ref, g2_ref, kk_ref, ka_ref,
                      rk_ref, kap_o, r_o, kt_o, bt_o, kh_o, bh_o, v_o, dec_o, g_o, bonus_o,
                      *, period, chunk, tiles_per_seq):
    p = p_ref[...]
    tm = p.shape[0]
    row = lax.broadcasted_iota(jnp.int32, (tm, 1), 0)
    if tiles_per_seq is None:
        shifted_in = ovr_ref[...]
    else:
        first = pl.program_id(0) % tiles_per_seq == 0
        shifted_in = jnp.where(first, ovr_ref[...], above_ref[SUBLANES - 1:SUBLANES, :])
    prev = jnp.where(row % period == 0, shifted_in, pltpu.roll(p, 1, 0))
    xm = p + (prev - p) * mu_ref[...]
    r = xm[:, :D_MODEL]
    k = xm[:, D_MODEL:2 * D_MODEL]
    v = xm[:, 2 * D_MODEL:3 * D_MODEL]
    lora = xm[:, 3 * D_MODEL:3 * D_MODEL + LORA_W]
    xg = xm[:, 3 * D_MODEL + LORA_W:]
    z = w0_ref[...] + jnp.dot(jnp.tanh(lora), w2_ref[...], precision=lax.Precision.HIGHEST,
                              preferred_element_type=F32)
    w_log = -jax.nn.softplus(-z) - 0.5
    a = jax.nn.sigmoid(a0_ref[...] + _dot(lora.astype(BF16), a2_ref[...]))
    g_o[...] = _dot(jax.nn.sigmoid(xg).astype(BF16), g2_ref[...])
    kk = k * kk_ref[...]
    k2 = k * (1.0 + (a - 1.0) * ka_ref[...])
    rk = r * k2 * rk_ref[...]
    lw = -jnp.exp(w_log)
    cum, tot, dec = _chunk_decay_sums(lw, chunk, dec_o.shape[0])
    dec_o[...] = jnp.exp(dec)
    g_in = jnp.exp(cum)
    g_ex = jnp.exp(cum - lw)
    g_inv = jnp.exp(-cum)
    g_end = jnp.exp(tot - cum)
    ones3 = _head_ones3()
    for pr in range(N_PAIRS):
        sl = slice(pr * PAIR_W, (pr + 1) * PAIR_W)
        kks = kk[:, sl]
        kap = kks / jnp.maximum(jnp.sqrt(_head_sums(kks * kks, ones3)), 1e-12)
        b = kap * a[:, sl]
        kap_o[pr] = (kap * g_ex[:, sl]).astype(BF16)
        r_o[pr] = r[:, sl] * g_in[:, sl]
        kt_o[pr] = (k2[:, sl] * g_inv[:, sl]).astype(BF16)
        bt_o[pr] = (b * g_inv[:, sl]).astype(BF16)
        kh_o[pr] = (k2[:, sl] * g_end[:, sl]).astype(BF16)
        bh_o[pr] = (b * g_end[:, sl]).astype(BF16)
        v_o[pr] = v[:, sl].astype(BF16)
        bonus_o[:, sl] = _head_sums(rk[:, sl], ones3) * v[:, sl]


def _rwkv_prep(p_rwkv, ovr, rw, *, tm, period, chunk):
    rows = p_rwkv.shape[0]
    n_dec = max(8, tm // chunk)
    if ovr.shape[1] == 1:
        tiles_per_seq = rows // ovr.shape[0] // tm
        assert period == tm
        ovr_spec = pl.BlockSpec((None, 1, SHIFT_W), lambda i: (i // tiles_per_seq, 0, 0))
    else:
        tiles_per_seq = None
        ovr_spec = pl.BlockSpec((None, tm, SHIFT_W), lambda i: (0, i, 0))
    above_spec = pl.BlockSpec((SUBLANES, SHIFT_W), lambda i: (jnp.maximum(i * (tm // SUBLANES) - 1, 0), 0))
    full = lambda a: pl.BlockSpec(a.shape, lambda i: (0,) * a.ndim)
    pm = lambda dt: jax.ShapeDtypeStruct((N_PAIRS, rows, PAIR_W), dt)
    rm = jax.ShapeDtypeStruct((rows, D_MODEL), F32)
    pm_spec = pl.BlockSpec((N_PAIRS, tm, PAIR_W), lambda i: (0, i, 0))
    rm_spec = pl.BlockSpec((tm, D_MODEL), lambda i: (i, 0))
    consts = (rw["mu"], rw["w0"], rw["w2"], rw["a0"], rw["a2"], rw["g2"], rw["k_k"], rw["k_a"], rw["r_k"])
    outs = pl.pallas_call(
        functools.partial(_rwkv_prep_kernel, period=period, chunk=chunk, tiles_per_seq=tiles_per_seq),
        out_shape=(pm(BF16), pm(F32)) + (pm(BF16),) * 5
        + (jax.ShapeDtypeStruct((rows // tm, n_dec, D_MODEL), F32), rm, rm),
        grid=(rows // tm,),
        in_specs=[pl.BlockSpec((tm, SHIFT_W), lambda i: (i, 0)), above_spec, ovr_spec]
        + [full(a) for a in consts],
        out_specs=(pm_spec,) * 7 + (pl.BlockSpec((None, n_dec, D_MODEL), lambda i: (i, 0, 0)), rm_spec, rm_spec),
        compiler_params=_cparams("parallel"),
        name="rwkv_prep",
    )(p_rwkv, p_rwkv, ovr, *consts)
    dec = outs[7][:, :tm // chunk].reshape(rows // chunk, 1, D_MODEL)
    return outs[:7], dec, outs[8], outs[9]


def _rwkv_weights(mu_shift, w0, w2_decay, a0, a2, g2, k_k, k_a, r_k):
    zeros = jnp.zeros((DECAY_LORA, D_MODEL), F32)
    return dict(
        mu=mu_shift.reshape(1, SHIFT_W), w0=w0.reshape(1, D_MODEL), a0=a0.reshape(1, D_MODEL),
        w2=jnp.concatenate([w2_decay, zeros], axis=0),
        a2=jnp.concatenate([zeros, a2], axis=0).astype(BF16),
        g2=g2.astype(BF16), k_k=k_k.reshape(1, D_MODEL), k_a=k_a.reshape(1, D_MODEL),
        r_k=r_k.reshape(1, D_MODEL))


INV_BASE = 8


def _unit_lower_inverses(l_ms, ti, tj):
    cl = l_ms[0].shape[0]
    base = min(INV_BASE, cl)
    eye = jnp.where(ti == tj, 1.0, 0.0)
    diag_blk = ti // base == tj // base
    l_bs = [jnp.where(diag_blk, l_m, 0.0) for l_m in l_ms]
    t_ms = [eye - l_b for l_b in l_bs]
    for _ in range(base.bit_length() - 2):
        l_bs = [_INV_MM(_dot, l_b, l_b) for l_b in l_bs]
        t_ms = [t_m + _INV_MM(_dot, t_m, l_b) for t_m, l_b in zip(t_ms, l_bs)]
    size = base
    while size < cl:
        lower_left = (ti // (2 * size) == tj // (2 * size)) & (ti // size != tj // size)
        tmp = [_INV_MM(_dot, t_m, jnp.where(lower_left, l_m, 0.0)) for t_m, l_m in zip(t_ms, l_ms)]
        t_ms = [t_m - _INV_MM(_dot, x, t_m) for t_m, x in zip(t_ms, tmp)]
        size *= 2
    return t_ms


def _rwkv_chunk_kernel(kap_ref, r_ref, kt_ref, bt_ref, kh_ref, bh_ref, v_ref, dec_ref, s0_ref, o_ref, so_ref,
                       st_scr):
    c = pl.program_id(1)
    n_c = pl.num_programs(1)
    cl = r_ref.shape[1]
    hd = HEAD_DIM

    @pl.when(c == 0)
    def _():
        z = jnp.zeros((hd, hd), F32)
        for pr in range(N_PAIRS):
            top = jnp.concatenate([s0_ref[2 * pr], z], axis=1)
            bot = jnp.concatenate([z, s0_ref[2 * pr + 1]], axis=1)
            st_scr[pr] = jnp.concatenate([top, bot], axis=0)

    ti = lax.broadcasted_iota(jnp.int32, (cl, cl), 0)
    tj = lax.broadcasted_iota(jnp.int32, (cl, cl), 1)
    tril = ti >= tj
    stril = ti > tj
    head1 = lax.broadcasted_iota(jnp.int32, (cl, PAIR_W), 1) >= hd
    sr = lax.broadcasted_iota(jnp.int32, (PAIR_W, PAIR_W), 0)
    sc = lax.broadcasted_iota(jnp.int32, (PAIR_W, PAIR_W), 1)
    same_head = (sr // hd) == (sc // hd)

    pairs = range(N_PAIRS)
    hs = [(pr, h) for pr in pairs for h in range(2)]
    bf = lambda xs: [x.astype(BF16) for x in xs]
    r_t = [r_ref[pr] for pr in pairs]
    kap_b = [kap_ref[pr] for pr in pairs]
    kt_b = [kt_ref[pr] for pr in pairs]
    bt_b = [bt_ref[pr] for pr in pairs]
    v_b = [v_ref[pr] for pr in pairs]
    masks = (jnp.logical_not(head1), head1)
    r_h = [jnp.where(masks[h], r_t[pr], 0.0) for pr, h in hs]
    r_hb = bf(r_h)
    kap_hb = [jnp.where(masks[h], kap_b[pr], jnp.zeros_like(kap_b[pr])) for pr, h in hs]
    rows4 = [jnp.concatenate([kap_hb[2 * pr], r_hb[2 * pr], kap_hb[2 * pr + 1], r_hb[2 * pr + 1]], axis=0)
             for pr in pairs]
    vs_b = [_dot_nt(rows4[pr], bt_b[pr]) for pr in pairs]
    vs_k = [_dot_nt(rows4[pr], kt_b[pr]) for pr in pairs]
    blk4 = lambda x, i, h: x[(2 * h + i) * cl:(2 * h + i + 1) * cl]
    l_m = [jnp.where(stril, blk4(vs_b[pr], 0, h), 0.0) for pr, h in hs]
    a_kk = [jnp.where(stril, blk4(vs_k[pr], 0, h), 0.0) for pr, h in hs]
    a_rk = [jnp.where(tril, blk4(vs_k[pr], 1, h), 0.0) for pr, h in hs]
    a_rb_b = bf([jnp.where(tril, blk4(vs_b[pr], 1, h), 0.0) for pr, h in hs])
    a_v = [_dot(jnp.concatenate(bf([a_kk[2 * pr], a_rk[2 * pr], a_kk[2 * pr + 1], a_rk[2 * pr + 1]]), axis=0),
                v_b[pr]) for pr in pairs]
    av = [blk4(a_v[pr], 0, h) for pr, h in hs]
    oi_h = [blk4(a_v[pr], 1, h) for pr, h in hs]
    t_b = bf(_unit_lower_inverses(l_m, ti, tj))
    wu = [_dot(t, jnp.concatenate([k, x], axis=1)) for t, k, x in zip(t_b, kap_hb, bf(av))]
    w_h = [x[:, :PAIR_W] for x in wu]
    u_h = [x[:, PAIR_W:] for x in wu]
    corr = [_dot(a, x.astype(BF16)) for a, x in zip(a_rb_b, wu)]
    rp_h = [r - x[:, :PAIR_W] for r, x in zip(r_h, corr)]
    oi_h = [o - x[:, PAIR_W:] for o, x in zip(oi_h, corr)]
    w_m = bf([w_h[2 * pr] + w_h[2 * pr + 1] for pr in pairs])
    u_m = bf([jnp.where(head1, u_h[2 * pr + 1], u_h[2 * pr]) for pr in pairs])
    rp = [rp_h[2 * pr] + rp_h[2 * pr + 1] for pr in pairs]
    oi = [jnp.where(head1, oi_h[2 * pr + 1], oi_h[2 * pr]) for pr in pairs]
    bh_b = [bh_ref[pr] for pr in pairs]
    kh_b = [kh_ref[pr] for pr in pairs]
    wtb = [jnp.where(same_head, _dot_tn(w_m[pr], bh_b[pr]), 0.0) for pr in pairs]
    n_t = [jnp.where(same_head, _dot_tn(v_b[pr], kh_b[pr]) - _dot_tn(u_m[pr], bh_b[pr]), 0.0) for pr in pairs]
    s = [st_scr[pr] for pr in pairs]
    o_new = [_mm3(_dot_nt, rp[pr], s[pr]) + oi[pr] for pr in pairs]
    s_new = [s[pr] * dec_ref[:, pr * PAIR_W:(pr + 1) * PAIR_W] - _mm3(_dot, s[pr], wtb[pr]) + n_t[pr]
             for pr in pairs]
    for pr in pairs:
        o_ref[pr] = o_new[pr]
        st_scr[pr] = s_new[pr]

    @pl.when(c == n_c - 1)
    def _():
        for pr in range(N_PAIRS):
            s = st_scr[pr]
            so_ref[2 * pr] = s[:hd, :hd]
            so_ref[2 * pr + 1] = s[hd:, hd:]


def _rwkv_chunks(pm_inputs, dec, s0, *, n_seq, chunk):
    rows = pm_inputs[0].shape[1]
    n_c = rows // n_seq // chunk
    pm_spec = pl.BlockSpec((N_PAIRS, chunk, PAIR_W), lambda s, c: (0, s * n_c + c, 0))
    st_spec = pl.BlockSpec((None, N_HEADS, HEAD_DIM, HEAD_DIM), lambda s, c: (s, 0, 0, 0))
    return pl.pallas_call(
        _rwkv_chunk_kernel,
        out_shape=(jax.ShapeDtypeStruct((N_PAIRS, rows, PAIR_W), F32),
                   jax.ShapeDtypeStruct((n_seq, N_HEADS, HEAD_DIM, HEAD_DIM), F32)),
        grid=(n_seq, n_c),
        in_specs=[pm_spec] * 7 + [pl.BlockSpec((None, 1, D_MODEL), lambda s, c: (s * n_c + c, 0, 0)), st_spec],
        out_specs=(pm_spec, st_spec),
        scratch_shapes=[pltpu.VMEM((N_PAIRS, PAIR_W, PAIR_W), F32)],
        compiler_params=_cparams("parallel", "arbitrary"),
        name="rwkv_chunks",
    )(*pm_inputs, dec, s0)


def _mixer_out_kernel(x_ref, gt_ref, npost_ref, o_ref, bonus_ref, g_ref, attn_ref, ga_ref, gb_ref,
                      lnw_ref, lnb_ref, wor_ref, wom_ref, wout_ref, y_ref):
    ones3 = _head_ones3()
    parts = []
    for pr in range(N_PAIRS):
        o = o_ref[pr]
        mu = _head_sums(o, ones3) * (1.0 / HEAD_DIM)
        d = o - mu
        var = _head_sums(d * d, ones3) * (1.0 / HEAD_DIM)
        parts.append(d * lax.rsqrt(var + LNX_EPS))
    on = jnp.concatenate(parts, axis=1) * lnw_ref[...] + lnb_ref[...]
    ya = _dot(((on + bonus_ref[...]) * g_ref[...]).astype(BF16), wor_ref[...])
    yb = _dot(attn_ref[...], wom_ref[...])
    y = jax.nn.sigmoid(ga_ref[...]) * ya + jax.nn.sigmoid(gb_ref[...]) * yb
    y = _dot(y.astype(BF16), wout_ref[...])
    y_ref[...] = x_ref[...] + gt_ref[...] * _rms(y, npost_ref[...])


def _mixer_out(x, mod, which, n_post, o_pm, bonus, g, attn, gates, lnx_w, lnx_b, w_o_rwkv, w_o_moba, w_out,
               *, tm, tiles_per_group):
    rows = x.shape[0]
    row_spec = pl.BlockSpec((tm, D_MODEL), lambda i: (i, 0))
    w_spec = pl.BlockSpec((D_MODEL, D_MODEL), lambda i: (0, 0))
    return pl.pallas_call(
        _mixer_out_kernel,
        out_shape=jax.ShapeDtypeStruct((rows, D_MODEL), F32),
        grid=(rows // tm,),
        in_specs=[
            row_spec,
            _mod_spec(mod, which, tm, tiles_per_group),
            _vec_spec(),
            pl.BlockSpec((N_PAIRS, tm, PAIR_W), lambda i: (0, i, 0)),
            row_spec, row_spec, row_spec,
            pl.BlockSpec((tm, D_MODEL), lambda i: (i, 0)),
            pl.BlockSpec((tm, D_MODEL), lambda i: (i, 1)),
            _vec_spec(), _vec_spec(), w_spec, w_spec, w_spec,
        ],
        out_specs=row_spec,
        compiler_params=_cparams("parallel"),
        name="mixer_out",
    )(x, mod, n_post, o_pm, bonus, g, attn, gates, gates, lnx_w, lnx_b, w_o_rwkv, w_o_moba, w_out)


def _rope_rows_kernel(qkv_ref, cos_ref, sin_ref, q_ref, k_ref):
    cos = cos_ref[...]
    sin = sin_ref[...]
    for pr in range(N_PAIRS):
        sl = slice(pr * PAIR_W, (pr + 1) * PAIR_W)
        q_ref[:, sl] = _rope_pair(qkv_ref[:, sl], cos, sin)
        k_ref[:, sl] = _rope_pair(qkv_ref[:, D_MODEL + pr * PAIR_W:D_MODEL + (pr + 1) * PAIR_W], cos, sin)


def _rope_rows(qkv, cos_t, sin_t):
    rows = qkv.shape[0]
    out = jax.ShapeDtypeStruct((rows, D_MODEL), F32)
    return pl.pallas_call(_rope_rows_kernel, out_shape=(out, out), name="rope_rows")(qkv, cos_t, sin_t)


TOK_SLOTS = 8
Q_TOKENS = 4

PAGES_PER_STEP = 8
PAGES_PER_BLOCK = MOBA_BLOCK // PAGE_SIZE


def _moba_sample_kernel(pt_ref, q_ref, kn_ref, vn_ref, *refs, n_seq, n_pages, n_tok):
    pps = PAGES_PER_STEP
    kp_refs, vp_refs = refs[:pps], refs[pps:2 * pps]
    o_ref, q2_scr, s_scr, pn_scr, acc_scr, l_scr = refs[2 * pps:]
    b = pl.program_id(0)
    n = pl.program_id(1)
    n_steps = pl.num_programs(1)
    rows = N_HEADS * Q_TOKENS
    n_blk = n_pages // PAGES_PER_BLOCK
    blk_per_step = pps // PAGES_PER_BLOCK
    scale = HEAD_DIM ** -0.5
    row_head = lax.broadcasted_iota(jnp.int32, (rows, D_MODEL), 0) % N_HEADS
    lane_head = lax.broadcasted_iota(jnp.int32, (rows, D_MODEL), 1) // HEAD_DIM
    slot_k = b % 2
    slot_v = (b + 1) % 2

    @pl.when((n == 0) & (b < n_seq))
    def _():
        q_rep = jnp.concatenate([jnp.broadcast_to(q_ref[t:t + 1, :], (N_HEADS, D_MODEL)) for t in range(Q_TOKENS)],
                                axis=0)
        q_bd = jnp.where(row_head == lane_head, q_rep, 0.0)
        hi, lo = _hi_lo(q_bd)
        q2_scr[:rows] = hi
        q2_scr[rows:] = lo

    @pl.when(b < n_seq)
    def _():
        kh, kl = _hi_lo(jnp.concatenate([r[...] for r in kp_refs], axis=1))
        top = _dot(q2_scr[...], kh)
        s = top[:rows] + top[rows:] + _dot(q2_scr[:rows], kl)
        for i in range(blk_per_step):
            s_scr[slot_k, n * blk_per_step + i] = s[:, i * MOBA_BLOCK:(i + 1) * MOBA_BLOCK]

    @pl.when((n == n_steps - 1) & (b < n_seq))
    def _():
        cols = [jnp.sum(s_scr[slot_k, i], axis=1, keepdims=True) for i in range(n_blk)]
        gate = jnp.concatenate(cols, axis=1)
        col = lax.broadcasted_iota(jnp.int32, gate.shape, 1)
        cnt = jnp.zeros(gate.shape, jnp.int32)
        for n in range(n_blk):
            gn = gate[:, n:n + 1]
            cnt = cnt + jnp.where((gn > gate) | ((gn == gate) & (n < col)), 1, 0)
        bias = jnp.where(cnt < MOBA_TOPK, 0.0, NEG_INF)
        knh, knl = _hi_lo(kn_ref[...])
        qh = q2_scr[:rows]
        s_new = (_dot_nt(qh, knh) + _dot_nt(q2_scr[rows:], knh) + _dot_nt(qh, knl)) * scale
        t_q = lax.broadcasted_iota(jnp.int32, s_new.shape, 0) // N_HEADS
        t_k = lax.broadcasted_iota(jnp.int32, s_new.shape, 1)
        s_new = jnp.where((t_k <= t_q) & (t_k < n_tok), s_new, NEG_INF)
        logits = [s_scr[slot_k, i] * scale + bias[:, i:i + 1] for i in range(n_blk)]
        mx = functools.reduce(jnp.maximum, logits)
        m = jnp.maximum(jnp.max(mx, axis=1, keepdims=True), jnp.max(s_new, axis=1, keepdims=True))
        p_new = jnp.exp(s_new - m)
        ps = [jnp.exp(x - m) for x in logits]
        for i in range(n_blk):
            s_scr[slot_k, i] = ps[i]
        tot = functools.reduce(jnp.add, ps)
        l_scr[slot_k] = jnp.sum(tot, axis=1, keepdims=True) + jnp.sum(p_new, axis=1, keepdims=True)
        pn_scr[slot_k] = p_new

    @pl.when(b >= 1)
    def _():
        @pl.when(n == 0)
        def _():
            acc_scr[...] = _dot(pn_scr[slot_v], vn_ref[...])

        for i in range(blk_per_step):
            vb = jnp.concatenate([r[...] for r in vp_refs[i * PAGES_PER_BLOCK:(i + 1) * PAGES_PER_BLOCK]], axis=1)
            acc_scr[...] += _dot_nt(s_scr[slot_v, n * blk_per_step + i].astype(BF16), vb.astype(BF16))

        @pl.when(n == n_steps - 1)
        def _():
            acc = jnp.where(row_head == lane_head, acc_scr[...] / l_scr[slot_v], 0.0)
            tok = [jnp.sum(acc[t * N_HEADS:(t + 1) * N_HEADS], axis=0, keepdims=True) for t in range(Q_TOKENS)]
            o_ref[...] = jnp.concatenate(tok + [jnp.zeros((TOK_SLOTS - Q_TOKENS, D_MODEL), F32)], axis=0)


def _moba_sample(page_table, q_rot, k_new, v_new, cache_kt, cache_vt, *, n_tok):
    n_seq, n_pages = page_table.shape
    pps = PAGES_PER_STEP
    assert n_pages % pps == 0 and pps % PAGES_PER_BLOCK == 0 and n_tok <= Q_TOKENS
    rows = N_HEADS * Q_TOKENS
    n_blk = n_pages // PAGES_PER_BLOCK
    cur = lambda b: jnp.minimum(b, n_seq - 1)
    prev = lambda b: jnp.maximum(b - 1, 0)
    cur_spec = pl.BlockSpec((None, TOK_SLOTS, D_MODEL), lambda b, n, pt: (cur(b), 0, 0))
    prev_spec = pl.BlockSpec((None, TOK_SLOTS, D_MODEL), lambda b, n, pt: (prev(b), 0, 0))

    def page_spec(seq_of, i):
        return pl.BlockSpec((None, D_MODEL, PAGE_SIZE),
                            lambda b, n, pt: (pt[seq_of(b) * n_pages + n * pps + i], 0, 0))

    return pl.pallas_call(
        functools.partial(_moba_sample_kernel, n_seq=n_seq, n_pages=n_pages, n_tok=n_tok),
        out_shape=jax.ShapeDtypeStruct((n_seq, TOK_SLOTS, D_MODEL), F32),
        grid_spec=pltpu.PrefetchScalarGridSpec(
            num_scalar_prefetch=1,
            grid=(n_seq + 1, n_pages // pps),
            in_specs=[cur_spec, cur_spec, prev_spec]
            + [page_spec(cur, i) for i in range(pps)] + [page_spec(prev, i) for i in range(pps)],
            out_specs=prev_spec,
            scratch_shapes=[
                pltpu.VMEM((2 * rows, D_MODEL), BF16),
                pltpu.VMEM((2, n_blk, rows, MOBA_BLOCK), F32),
                pltpu.VMEM((2, rows, TOK_SLOTS), F32),
                pltpu.VMEM((rows, D_MODEL), F32),
                pltpu.VMEM((2, rows, 1), F32),
            ],
        ),
        compiler_params=_cparams("arbitrary", "arbitrary"),
        name="moba_sample",
    )(page_table.reshape(-1), q_rot, k_new, v_new, *([cache_kt] * pps), *([cache_vt] * pps))


FFN_TM = 1024
INPROJ_TM = 512
RWKV_TM = 256
PROMPT_CHUNK = 64
SAMPLE_CHUNK = 16


def _layer(x, mod, lw, *, group_rows, shift_ovr, shift_period, prep_chunk, attend, rwkv_scan):
    tm_ffn, tm_in = min(FFN_TM, group_rows), min(INPROJ_TM, group_rows)
    x1 = _ffn(x, mod, 0, lw["n1_pre"], lw["n1_post"], lw["w1_gu"], lw["w1_down"], tm=tm_ffn,
              tiles_per_group=group_rows // tm_ffn)
    p_rwkv, qkv, gates = _inproj(x1, mod, 3, lw["n2_pre"], lw["w_in"], tm=tm_in,
                                 tiles_per_group=group_rows // tm_in)
    attn, extras = attend(qkv)
    pm, dec, g, bonus = _rwkv_prep(p_rwkv, shift_ovr, lw["rwkv"], tm=RWKV_TM, period=shift_period,
                                   chunk=prep_chunk)
    o_pm, state = rwkv_scan(pm, dec)
    x2 = _mixer_out(x1, mod, 5, lw["n2_post"], o_pm, bonus, g, attn, gates, lw["lnx_w"], lw["lnx_b"],
                    lw["w_o_rwkv"], lw["w_o_moba"], lw["w_out"], tm=RWKV_TM,
                    tiles_per_group=group_rows // RWKV_TM)
    y = _ffn(x2, mod, 6, lw["n3_pre"], lw["n3_post"], lw["w3_gu"], lw["w3_down"], tm=tm_ffn,
             tiles_per_group=group_rows // tm_ffn)
    return y, p_rwkv, qkv, state, extras


def kernel(x_prompt, x_sample, cache_k, cache_v, state_rwkv, state_shift, page_table, c_prompt, c_sample, w_ada, b_ada, n1_pre, n1_post, w1_gu, w1_down, n2_pre, n2_post, w_in, mu_shift, w0, w2_decay, a0, a2, g2, k_k, k_a, r_k, lnx_w, lnx_b, w_o_rwkv, w_o_moba, w_out, n3_pre, n3_post, w3_gu, w3_down):
    assert w_ada.shape[0] == 1, "single layer"
    bp, seq, _ = x_prompt.shape
    bs, ts, _ = x_sample.shape
    n_pool = cache_k.shape[1]
    n_pages = page_table.shape[1]
    rows_p, rows_s = bp * seq, bs * ts
    assert seq % FFN_TM == 0 and seq % MOBA_BLOCK == 0 and rows_s % INPROJ_TM == 0 and ts <= SAMPLE_CHUNK

    lw = dict(
        n1_pre=n1_pre, n1_post=n1_post, n2_pre=n2_pre, n2_post=n2_post, n3_pre=n3_pre, n3_post=n3_post,
        w1_gu=w1_gu[0].astype(BF16), w1_down=w1_down[0].astype(BF16),
        w3_gu=w3_gu[0].astype(BF16), w3_down=w3_down[0].astype(BF16),
        w_in=_split_w_in(w_in[0]),
        rwkv=_rwkv_weights(mu_shift[0], w0[0], w2_decay[0], a0[0], a2[0], g2[0], k_k[0], k_a[0], r_k[0]),
        lnx_w=lnx_w, lnx_b=lnx_b,
        w_o_rwkv=w_o_rwkv[0].astype(BF16), w_o_moba=w_o_moba[0].astype(BF16), w_out=w_out[0].astype(BF16),
    )

    n_c = bp + bs
    c_all = jnp.concatenate([c_prompt, c_sample, jnp.zeros((-n_c % 8, D_MODEL), F32)], axis=0)
    mod = _ada(c_all, w_ada[0], b_ada[0])
    mod_p = mod[:bp].reshape(bp, N_ADA, 1, D_MODEL).transpose(1, 0, 2, 3)
    mod_s = jnp.repeat(mod[bp:n_c].reshape(bs, N_ADA, D_MODEL), ts, axis=0).transpose(1, 0, 2)[:, None]

    cos_p, sin_p = _rope_tables(jnp.arange(seq, dtype=jnp.int32))

    def attend_p(qkv):
        attn, k_rot, v = _moba_prompt(qkv, cos_p, sin_p, bp, seq)
        return attn.reshape(rows_p, D_MODEL), (k_rot, v)

    def scan_p(pm, dec):
        s0 = jnp.zeros((bp, N_HEADS, HEAD_DIM, HEAD_DIM), F32)
        return _rwkv_chunks(pm, dec, s0, n_seq=bp, chunk=PROMPT_CHUNK)

    y_p, p_p, _, state_p, (k_p, v_p) = _layer(
        x_prompt.reshape(rows_p, D_MODEL), mod_p, lw, group_rows=seq,
        shift_ovr=jnp.zeros((bp, 1, SHIFT_W), F32), shift_period=RWKV_TM, prep_chunk=PROMPT_CHUNK,
        attend=attend_p, rwkv_scan=scan_p)

    pos_s = n_pages * PAGE_SIZE + jnp.arange(ts, dtype=jnp.int32)
    cos_s, sin_s = (jnp.tile(t, (bs, 1)) for t in _rope_tables(pos_s))
    pad_tok = lambda a: jnp.pad(a.reshape(bs, ts, D_MODEL), ((0, 0), (0, TOK_SLOTS - ts), (0, 0)))
    cache_kt = cache_k[0].transpose(0, 2, 3, 1).reshape(n_pool, D_MODEL, PAGE_SIZE)
    cache_vt = cache_v[0].transpose(0, 2, 3, 1).reshape(n_pool, D_MODEL, PAGE_SIZE)

    def attend_s(qkv):
        q_rot, k_rot = _rope_rows(qkv, cos_s, sin_s)
        v = qkv[:, 2 * D_MODEL:]
        attn = _moba_sample(page_table, pad_tok(q_rot), pad_tok(k_rot), pad_tok(v), cache_kt, cache_vt, n_tok=ts)
        return attn[:, :ts].reshape(rows_s, D_MODEL).astype(BF16), (k_rot, v)

    def scan_s(pm, dec):
        pad = lambda a: jnp.pad(a.reshape(N_PAIRS, bs, ts, PAIR_W),
                                ((0, 0), (0, 0), (0, SAMPLE_CHUNK - ts), (0, 0))).reshape(N_PAIRS, -1, PAIR_W)
        o_pm, state = _rwkv_chunks([pad(a) for a in pm], dec, state_rwkv[0], n_seq=bs, chunk=SAMPLE_CHUNK)
        o_pm = o_pm.reshape(N_PAIRS, bs, SAMPLE_CHUNK, PAIR_W)[:, :, :ts].reshape(N_PAIRS, rows_s, PAIR_W)
        return o_pm, state

    y_s, p_s, _, state_s, (k_s, v_s) = _layer(
        x_sample.reshape(rows_s, D_MODEL), mod_s, lw, group_rows=rows_s,
        shift_ovr=jnp.repeat(state_shift[0], ts, axis=0)[None], shift_period=ts, prep_chunk=ts,
        attend=attend_s, rwkv_scan=scan_s)

    heads = lambda a, b, t: a.reshape(1, b, t, N_HEADS, HEAD_DIM)
    return (
        y_p.reshape(bp, seq, D_MODEL),
        y_s.reshape(bs, ts, D_MODEL),
        heads(k_p, bp, seq), heads(v_p, bp, seq),
        state_p[None], p_p.reshape(bp, seq, SHIFT_W)[:, -1][None],
        heads(k_s, bs, ts), heads(v_s, bs, ts),
        state_s[None], p_s.reshape(bs, ts, SHIFT_W)[:, -1][None],
    )
```

```python
import functools

import jax
import jax.numpy as jnp
from jax import lax
from jax.experimental import pallas as pl
from jax.experimental.pallas import tpu as pltpu

D_MODEL = 1024
HEAD_DIM = 64
N_HEADS = D_MODEL // HEAD_DIM
PAIR_W = 2 * HEAD_DIM
N_PAIRS = N_HEADS // 2
DECAY_LORA = 64
AAA_LORA = 64
GATE_LORA = 128
SHIFT_W = 3 * D_MODEL + DECAY_LORA + AAA_LORA + GATE_LORA
IN_W = SHIFT_W + 3 * D_MODEL + 2 * D_MODEL
D_FF = 11 * D_MODEL // 4
N_ADA = 9
MACARON_W = 0.5
RMS_EPS = 1e-6
LNX_EPS = 64e-5
MOBA_BLOCK = 256
MOBA_TOPK = 3
PAGE_SIZE = 128
ROPE_THETA = 10000.0

SUBLANES = 8
VMEM_LIMIT = 56 * 1024 * 1024

BF16 = jnp.bfloat16
F32 = jnp.float32
NEG_INF = float("-inf")


def _cparams(*sem):
    return pltpu.CompilerParams(dimension_semantics=sem, vmem_limit_bytes=VMEM_LIMIT)


def _rms(x, g):
    return x * lax.rsqrt(jnp.mean(x * x, axis=-1, keepdims=True) + RMS_EPS) * g


def _dot(a, b):
    return jnp.dot(a, b, preferred_element_type=F32)


def _dot_nt(a, b):
    return lax.dot_general(a, b, (((1,), (1,)), ((), ())), preferred_element_type=F32)


def _dot_tn(a, b):
    return lax.dot_general(a, b, (((0,), (0,)), ((), ())), preferred_element_type=F32)


ADA_TN = 1536


def _ada_kernel(c_ref, w_ref, b_ref, o_ref):
    c = c_ref[...]
    a = (c * jax.nn.sigmoid(c)).astype(BF16)
    o_ref[...] = _dot(a, w_ref[...].astype(BF16)) + b_ref[...]


def _ada(c_all, w_ada, b_ada):
    rows = c_all.shape[0]
    n_out = w_ada.shape[1]
    return pl.pallas_call(
        _ada_kernel,
        out_shape=jax.ShapeDtypeStruct((rows, n_out), F32),
        grid=(n_out // ADA_TN,),
        in_specs=[
            pl.BlockSpec((rows, D_MODEL), lambda j: (0, 0)),
            pl.BlockSpec((D_MODEL, ADA_TN), lambda j: (0, j)),
            pl.BlockSpec((1, ADA_TN), lambda j: (0, j)),
        ],
        out_specs=pl.BlockSpec((rows, ADA_TN), lambda j: (0, j)),
        compiler_params=_cparams("arbitrary"),
        name="ada_mod",
    )(c_all, w_ada, b_ada.reshape(1, n_out))


def _mod_spec(mod, which, tm, tiles_per_group):
    rows_in_group = mod.shape[2]
    if rows_in_group == 1:
        return pl.BlockSpec((None, None, 1, D_MODEL), lambda i, *_: (which, i // tiles_per_group, 0, 0))
    return pl.BlockSpec((None, None, tm, D_MODEL), lambda i, *_: (which, 0, i, 0))


def _vec_spec():
    return pl.BlockSpec((1, D_MODEL), lambda i, *_: (0, 0))


FFN_TF = D_FF // 2
FFN_SUB = 768


def _ffn_kernel(x_ref, sh_ref, sc_ref, gt_ref, npre_ref, npost_ref, wg_ref, wu_ref, wd_ref, o_ref, h_scr, acc_scr):
    j = pl.program_id(1)

    @pl.when(j == 0)
    def _():
        h = _rms(x_ref[...], npre_ref[...]) * (1.0 + sc_ref[...]) + sh_ref[...]
        h_scr[...] = h.astype(BF16)
        acc_scr[...] = jnp.zeros_like(acc_scr)

    h = h_scr[...]
    for lo in range(0, FFN_TF, FFN_SUB):
        cols = slice(lo, min(lo + FFN_SUB, FFN_TF))
        g = _dot(h, wg_ref[:, cols])
        u = _dot(h, wu_ref[:, cols])
        a = (g * jax.nn.sigmoid(g) * u).astype(BF16)
        acc_scr[...] += _dot(a, wd_ref[cols, :])

    @pl.when(j == pl.num_programs(1) - 1)
    def _():
        y = _rms(acc_scr[...], npost_ref[...])
        o_ref[...] = x_ref[...] + MACARON_W * gt_ref[...] * y


def _ffn(x, mod, which0, n_pre, n_post, wgu, wd, *, tm, tiles_per_group):
    rows = x.shape[0]
    nf = D_FF // FFN_TF
    row_spec = pl.BlockSpec((tm, D_MODEL), lambda i, j: (i, 0))
    return pl.pallas_call(
        _ffn_kernel,
        out_shape=jax.ShapeDtypeStruct((rows, D_MODEL), F32),
        grid=(rows // tm, nf),
        in_specs=[
            row_spec,
            _mod_spec(mod, which0, tm, tiles_per_group),
            _mod_spec(mod, which0 + 1, tm, tiles_per_group),
            _mod_spec(mod, which0 + 2, tm, tiles_per_group),
            _vec_spec(),
            _vec_spec(),
            pl.BlockSpec((D_MODEL, FFN_TF), lambda i, j: (0, j)),
            pl.BlockSpec((D_MODEL, FFN_TF), lambda i, j: (0, j + nf)),
            pl.BlockSpec((FFN_TF, D_MODEL), lambda i, j: (j, 0)),
        ],
        out_specs=row_spec,
        scratch_shapes=[pltpu.VMEM((tm, D_MODEL), BF16), pltpu.VMEM((tm, D_MODEL), F32)],
        compiler_params=_cparams("parallel", "arbitrary"),
        name="ffn",
    )(x, mod, mod, mod, n_pre, n_post, wgu, wgu, wd)


INPROJ_WIDTHS = (SHIFT_W, 3 * D_MODEL, 2 * D_MODEL)
INPROJ_SPLIT = 2


def _inproj_kernel(x_ref, sh_ref, sc_ref, npre_ref, w0_ref, w1_ref, w2_ref, o0_ref, o1_ref, o2_ref, h_scr):
    j = pl.program_id(1)

    @pl.when(j == 0)
    def _():
        h = _rms(x_ref[...], npre_ref[...]) * (1.0 + sc_ref[...]) + sh_ref[...]
        h_scr[...] = h.astype(BF16)

    for sec, (w_ref, o_ref) in enumerate(((w0_ref, o0_ref), (w1_ref, o1_ref), (w2_ref, o2_ref))):
        @pl.when(j // INPROJ_SPLIT == sec)
        def _(w_ref=w_ref, o_ref=o_ref):
            o_ref[...] = _dot(h_scr[...], w_ref[...])


def _inproj(x, mod, which0, n_pre, w_secs, *, tm, tiles_per_group):
    rows = x.shape[0]

    def sec_col(sec):
        return lambda i, j: jnp.clip(j - sec * INPROJ_SPLIT, 0, INPROJ_SPLIT - 1)

    w_specs, o_specs = [], []
    for sec, width in enumerate(INPROJ_WIDTHS):
        tn = width // INPROJ_SPLIT
        col = sec_col(sec)
        w_specs.append(pl.BlockSpec((D_MODEL, tn), lambda i, j, col=col: (0, col(i, j))))
        o_specs.append(pl.BlockSpec((tm, tn), lambda i, j, col=col: (i, col(i, j))))
    return pl.pallas_call(
        _inproj_kernel,
        out_shape=tuple(jax.ShapeDtypeStruct((rows, w), F32) for w in INPROJ_WIDTHS),
        grid=(rows // tm, len(INPROJ_WIDTHS) * INPROJ_SPLIT),
        in_specs=[
            pl.BlockSpec((tm, D_MODEL), lambda i, j: (i, 0)),
            _mod_spec(mod, which0, tm, tiles_per_group),
            _mod_spec(mod, which0 + 1, tm, tiles_per_group),
            _vec_spec(),
            *w_specs,
        ],
        out_specs=tuple(o_specs),
        scratch_shapes=[pltpu.VMEM((tm, D_MODEL), BF16)],
        compiler_params=_cparams("parallel", "arbitrary"),
        name="inproj",
    )(x, mod, mod, n_pre, *w_secs)


def _split_w_in(w_in):
    w = w_in.astype(BF16)
    c1 = SHIFT_W
    c2 = SHIFT_W + 3 * D_MODEL
    return w[:, :c1], w[:, c1:c2], w[:, c2:]


def _rope_tables(pos):
    half = HEAD_DIM // 2
    inv = ROPE_THETA ** (-jnp.arange(half, dtype=F32) / half)
    ang = pos.astype(F32)[:, None] * inv[None, :]
    cos, sin = jnp.cos(ang), jnp.sin(ang)
    return jnp.tile(cos, (1, 4)), jnp.tile(jnp.concatenate([-sin, sin], axis=1), (1, 2))


def _rope_pair(x, cos, sin_signed):
    lane = lax.broadcasted_iota(jnp.int32, x.shape, 1)
    first_half = (lane % HEAD_DIM) < HEAD_DIM // 2
    partner = jnp.where(first_half, pltpu.roll(x, PAIR_W - HEAD_DIM // 2, 1), pltpu.roll(x, HEAD_DIM // 2, 1))
    return x * cos + partner * sin_signed


LOG2_E = 1.4426950408889634
BLOCKS_PER_TRIP = 4
VT_ROWS = HEAD_DIM + 16


def _topk_bias(gate, n_past):
    nb = gate.shape[0]
    blk = lax.broadcasted_iota(jnp.int32, gate.shape, 0)
    cnt = jnp.zeros(gate.shape, jnp.int32)
    for jp in range(nb):
        gj = gate[jp:jp + 1, :]
        beats = (gj > gate) | ((gj == gate) & (jp < blk))
        cnt = cnt + jnp.where(beats & (jp < n_past), 1, 0)
    sel = (blk < n_past) & (cnt < MOBA_TOPK)
    return jnp.where(sel, 0.0, NEG_INF)


def _moba_prompt_kernel(q_ref, k_ref, v_ref, cos_ref, sin_ref, o_ref, ko_ref, vo_ref,
                        kbf_scr, vt_scr, kmean_scr, bias_scr):
    qi = pl.program_id(2)
    nb = kbf_scr.shape[0]
    blk = MOBA_BLOCK

    @pl.when(qi == 0)
    def _():
        def prep(j, carry):
            rows = pl.ds(pl.multiple_of(j * blk, blk), blk)
            kr = _rope_pair(k_ref[rows, :], cos_ref[rows, :], sin_ref[rows, :])
            ko_ref[rows, :] = kr
            kbf_scr[j] = kr.astype(BF16)
            kmean_scr[pl.ds(j, 1), :] = jnp.mean(kr, axis=0, keepdims=True)
            v = v_ref[rows, :]
            vo_ref[rows, :] = v
            vt = v.T.astype(BF16)
            for h in range(2):
                vt_scr[j, h, :HEAD_DIM, :] = vt[h * HEAD_DIM:(h + 1) * HEAD_DIM]
                vt_scr[j, h, HEAD_DIM:, :] = ones_rows
            return carry

        pad_row = lax.broadcasted_iota(jnp.int32, (VT_ROWS - HEAD_DIM, blk), 0)
        ones_rows = jnp.where(pad_row == 0, 1.0, 0.0).astype(BF16)
        lax.fori_loop(0, nb, prep, 0)

    rows_q = pl.ds(pl.multiple_of(qi * blk, blk), blk)
    q = _rope_pair(q_ref[...], cos_ref[rows_q, :], sin_ref[rows_q, :])
    lane = lax.broadcasted_iota(jnp.int32, q.shape, 1)
    kmean = kmean_scr[...]
    kidx = lax.broadcasted_iota(jnp.int32, (blk, blk), 0)
    qidx = lax.broadcasted_iota(jnp.int32, (blk, blk), 1)
    heads = range(2)
    hb = [(h, b) for h in heads for b in range(BLOCKS_PER_TRIP)]
    qh = [jnp.where(lane // HEAD_DIM == h, q, 0.0) for h in heads]
    km_lane = lax.broadcasted_iota(jnp.int32, kmean.shape, 1)
    kmean_by_head = jnp.concatenate([jnp.where(km_lane // HEAD_DIM == h, kmean, 0.0) for h in heads], axis=0)
    gate = lax.dot_general(kmean_by_head, q, (((1,), (1,)), ((), ())), precision=lax.Precision.HIGHEST,
                           preferred_element_type=F32)
    for h in heads:
        bias = _topk_bias(gate[h * nb:(h + 1) * nb], qi)
        for j in range(nb):
            bias_scr[h, j] = jnp.broadcast_to(bias[j:j + 1], (SUBLANES, blk))
    qs = [(qh[h] * (HEAD_DIM ** -0.5 * LOG2_E)).astype(BF16) for h in heads]

    def shifted_exp2(s, shift):
        x = s.reshape(blk // SUBLANES, SUBLANES, blk) + shift[None]
        return jnp.exp2(x).reshape(blk, blk).astype(BF16)

    k_own = kbf_scr[qi]
    s_own = [jnp.where(kidx <= qidx, _dot_nt(k_own, qs[h]), NEG_INF) for h in heads]
    m_own = [jnp.max(s_own[h], axis=0, keepdims=True) for h in heads]
    acc_own = [_dot(vt_scr[qi, h], jnp.exp2(s_own[h] - m_own[h]).astype(BF16)) for h in heads]

    def trip(u, carry):
        m, acc = carry
        js = [jnp.minimum(BLOCKS_PER_TRIP * u + b, nb - 1) for b in range(BLOCKS_PER_TRIP)]
        s = [_dot_nt(kbf_scr[js[b]], qs[h]) for h, b in hb]
        bias = [bias_scr[h, js[b]] for h, b in hb]
        bm = [jnp.max(x, axis=0, keepdims=True) + c[:1] for x, c in zip(s, bias)]
        m_new = [jnp.maximum(m[h], functools.reduce(jnp.maximum, bm[h * BLOCKS_PER_TRIP:(h + 1) * BLOCKS_PER_TRIP]))
                 for h in heads]
        alpha = [jnp.exp2(m[h] - m_new[h]) for h in heads]
        p = [shifted_exp2(x, c - m_new[h]) for x, c, (h, b) in zip(s, bias, hb)]
        pv = [_dot(vt_scr[js[b], h], x) for x, (h, b) in zip(p, hb)]
        acc = [alpha[h] * acc[h] + functools.reduce(jnp.add, pv[h * BLOCKS_PER_TRIP:(h + 1) * BLOCKS_PER_TRIP])
               for h in heads]
        return m_new, acc

    n_trips = (qi + BLOCKS_PER_TRIP - 1) // BLOCKS_PER_TRIP
    _, acc = lax.fori_loop(0, n_trips, trip, (m_own, acc_own))
    out = [acc[h][:HEAD_DIM] / acc[h][HEAD_DIM:HEAD_DIM + 1] for h in heads]
    o_ref[...] = jnp.concatenate(out, axis=0).T.astype(o_ref.dtype)


def _moba_prompt(qkv, cos_t, sin_t, n_seq, seq_len):
    proj3 = qkv.reshape(n_seq, seq_len, 3 * D_MODEL)
    nb = seq_len // MOBA_BLOCK
    qb, kb, vb = 0, N_PAIRS, 2 * N_PAIRS
    seq_spec = lambda col0: pl.BlockSpec((None, seq_len, PAIR_W), lambda b, hp, qi: (b, 0, col0 + hp))
    tab_spec = pl.BlockSpec((seq_len, PAIR_W), lambda b, hp, qi: (0, 0))
    return pl.pallas_call(
        _moba_prompt_kernel,
        out_shape=(
            jax.ShapeDtypeStruct((n_seq, seq_len, D_MODEL), BF16),
            jax.ShapeDtypeStruct((n_seq, seq_len, D_MODEL), F32),
            jax.ShapeDtypeStruct((n_seq, seq_len, D_MODEL), F32),
        ),
        grid=(n_seq, N_PAIRS, nb),
        in_specs=[
            pl.BlockSpec((None, MOBA_BLOCK, PAIR_W), lambda b, hp, qi: (b, qi, qb + hp)),
            seq_spec(kb),
            seq_spec(vb),
            tab_spec,
            tab_spec,
        ],
        out_specs=(
            pl.BlockSpec((None, MOBA_BLOCK, PAIR_W), lambda b, hp, qi: (b, qi, hp)),
            seq_spec(0),
            seq_spec(0),
        ),
        scratch_shapes=[
            pltpu.VMEM((nb, MOBA_BLOCK, PAIR_W), BF16),
            pltpu.VMEM((nb, 2, VT_ROWS, MOBA_BLOCK), BF16),
            pltpu.VMEM((nb, PAIR_W), F32),
            pltpu.VMEM((2, nb, SUBLANES, MOBA_BLOCK), F32),
        ],
        compiler_params=_cparams("parallel", "parallel", "arbitrary"),
        name="moba_prompt",
    )(proj3, proj3, proj3, cos_t, sin_t)


def _hi_lo(x):
    hi = x.astype(BF16)
    return hi, (x - hi.astype(F32)).astype(BF16)


def _mm3(dot_fn, a, b):
    ah, al = _hi_lo(a)
    bh, bl = _hi_lo(b)
    return dot_fn(ah, bh) + dot_fn(ah, bl) + dot_fn(al, bh)


def _mm1(dot_fn, a, b):
    return dot_fn(a.astype(BF16), b.astype(BF16))


_INV_MM = _mm1


def _head_ones3():
    r = lax.broadcasted_iota(jnp.int32, (3 * PAIR_W, PAIR_W), 0)
    c = lax.broadcasted_iota(jnp.int32, (3 * PAIR_W, PAIR_W), 1)
    return jnp.where((r % PAIR_W) // HEAD_DIM == c // HEAD_DIM, 1.0, 0.0).astype(BF16)


def _head_sums(x, ones3):
    hi = x.astype(BF16)
    r1 = x - hi.astype(F32)
    mid = r1.astype(BF16)
    lo = (r1 - mid.astype(F32)).astype(BF16)
    return _dot(jnp.concatenate([hi, mid, lo], axis=1), ones3)


LORA_W = DECAY_LORA + AAA_LORA


def _split3(x):
    hi = x.astype(BF16)
    r1 = x - hi.astype(F32)
    mid = r1.astype(BF16)
    return hi, mid, (r1 - mid.astype(F32)).astype(BF16)


def _chunk_decay_sums(lw, chunk, n_dec):
    tm = lw.shape[0]
    ci = lax.broadcasted_iota(jnp.int32, (tm, tm), 0)
    cj = lax.broadcasted_iota(jnp.int32, (tm, tm), 1)
    same = ci // chunk == cj // chunk
    di = lax.broadcasted_iota(jnp.int32, (n_dec, tm), 0)
    dj = lax.broadcasted_iota(jnp.int32, (n_dec, tm), 1)
    sel = jnp.concatenate([
        jnp.where(same & (ci >= cj), 1.0, 0.0),
        jnp.where(same, 1.0, 0.0),
        jnp.where(di == dj // chunk, 1.0, 0.0)], axis=0).astype(BF16)
    big = _dot(sel, jnp.concatenate(_split3(lw), axis=1))
    big = big[:, :D_MODEL] + big[:, D_MODEL:2 * D_MODEL] + big[:, 2 * D_MODEL:]
    return big[:tm], big[tm:2 * tm], big[2 * tm:]


def _rwkv_prep_kernel(p_ref, above_ref, ovr_ref, mu_ref, w0_ref, w2_ref, a0_ref, a2_ref, g2_ref, kk_ref, ka_ref,
                      rk_ref, kap_o, r_o, kt_o, bt_o, kh_o, bh_o, v_o, dec_o, g_o, bonus_o,
                      *, period, chunk, tiles_per_seq):
    p = p_ref[...]
    tm = p.shape[0]
    row = lax.broadcasted_iota(jnp.int32, (tm, 1), 0)
    if tiles_per_seq is None:
        shifted_in = ovr_ref[...]
    else:
        first = pl.program_id(0) % tiles_per_seq == 0
        shifted_in = jnp.where(first, ovr_ref[...], above_ref[SUBLANES - 1:SUBLANES, :])
    prev = jnp.where(row % period == 0, shifted_in, pltpu.roll(p, 1, 0))
    xm = p + (prev - p) * mu_ref[...]
    r = xm[:, :D_MODEL]
    k = xm[:, D_MODEL:2 * D_MODEL]
    v = xm[:, 2 * D_MODEL:3 * D_MODEL]
    lora = xm[:, 3 * D_MODEL:3 * D_MODEL + LORA_W]
    xg = xm[:, 3 * D_MODEL + LORA_W:]
    z = w0_ref[...] + jnp.dot(jnp.tanh(lora), w2_ref[...], precision=lax.Precision.HIGHEST,
                              preferred_element_type=F32)
    w_log = -jax.nn.softplus(-z) - 0.5
    a = jax.nn.sigmoid(a0_ref[...] + _dot(lora.astype(BF16), a2_ref[...]))
    g_o[...] = _dot(jax.nn.sigmoid(xg).astype(BF16), g2_ref[...])
    kk = k * kk_ref[...]
    k2 = k * (1.0 + (a - 1.0) * ka_ref[...])
    rk = r * k2 * rk_ref[...]
    lw = -jnp.exp(w_log)
    cum, tot, dec = _chunk_decay_sums(lw, chunk, dec_o.shape[0])
    dec_o[...] = jnp.exp(dec)
    g_in = jnp.exp(cum)
    g_ex = jnp.exp(cum - lw)
    g_inv = jnp.exp(-cum)
    g_end = jnp.exp(tot - cum)
    ones3 = _head_ones3()
    for pr in range(N_PAIRS):
        sl = slice(pr * PAIR_W, (pr + 1) * PAIR_W)
        kks = kk[:, sl]
        kap = kks / jnp.maximum(jnp.sqrt(_head_sums(kks * kks, ones3)), 1e-12)
        b = kap * a[:, sl]
        kap_o[pr] = (kap * g_ex[:, sl]).astype(BF16)
        r_o[pr] = r[:, sl] * g_in[:, sl]
        kt_o[pr] = (k2[:, sl] * g_inv[:, sl]).astype(BF16)
        bt_o[pr] = (b * g_inv[:, sl]).astype(BF16)
        kh_o[pr] = (k2[:, sl] * g_end[:, sl]).astype(BF16)
        bh_o[pr] = (b * g_end[:, sl]).astype(BF16)
        v_o[pr] = v[:, sl].astype(BF16)
        bonus_o[:, sl] = _head_sums(rk[:, sl], ones3) * v[:, sl]


def _rwkv_prep(p_rwkv, ovr, rw, *, tm, period, chunk):
    rows = p_rwkv.shape[0]
    n_dec = max(8, tm // chunk)
    if ovr.shape[1] == 1:
        tiles_per_seq = rows // ovr.shape[0] // tm
        assert period == tm
        ovr_spec = pl.BlockSpec((None, 1, SHIFT_W), lambda i: (i // tiles_per_seq, 0, 0))
    else:
        tiles_per_seq = None
        ovr_spec = pl.BlockSpec((None, tm, SHIFT_W), lambda i: (0, i, 0))
    above_spec = pl.BlockSpec((SUBLANES, SHIFT_W), lambda i: (jnp.maximum(i * (tm // SUBLANES) - 1, 0), 0))
    full = lambda a: pl.BlockSpec(a.shape, lambda i: (0,) * a.ndim)
    pm = lambda dt: jax.ShapeDtypeStruct((N_PAIRS, rows, PAIR_W), dt)
    rm = jax.ShapeDtypeStruct((rows, D_MODEL), F32)
    pm_spec = pl.BlockSpec((N_PAIRS, tm, PAIR_W), lambda i: (0, i, 0))
    rm_spec = pl.BlockSpec((tm, D_MODEL), lambda i: (i, 0))
    consts = (rw["mu"], rw["w0"], rw["w2"], rw["a0"], rw["a2"], rw["g2"], rw["k_k"], rw["k_a"], rw["r_k"])
    outs = pl.pallas_call(
        functools.partial(_rwkv_prep_kernel, period=period, chunk=chunk, tiles_per_seq=tiles_per_seq),
        out_shape=(pm(BF16), pm(F32)) + (pm(BF16),) * 5
        + (jax.ShapeDtypeStruct((rows // tm, n_dec, D_MODEL), F32), rm, rm),
        grid=(rows // tm,),
        in_specs=[pl.BlockSpec((tm, SHIFT_W), lambda i: (i, 0)), above_spec, ovr_spec]
        + [full(a) for a in consts],
        out_specs=(pm_spec,) * 7 + (pl.BlockSpec((None, n_dec, D_MODEL), lambda i: (i, 0, 0)), rm_spec, rm_spec),
        compiler_params=_cparams("parallel"),
        name="rwkv_prep",
    )(p_rwkv, p_rwkv, ovr, *consts)
    dec = outs[7][:, :tm // chunk].reshape(rows // chunk, 1, D_MODEL)
    return outs[:7], dec, outs[8], outs[9]


def _rwkv_weights(mu_shift, w0, w2_decay, a0, a2, g2, k_k, k_a, r_k):
    zeros = jnp.zeros((DECAY_LORA, D_MODEL), F32)
    return dict(
        mu=mu_shift.reshape(1, SHIFT_W), w0=w0.reshape(1, D_MODEL), a0=a0.reshape(1, D_MODEL),
        w2=jnp.concatenate([w2_decay, zeros], axis=0),
        a2=jnp.concatenate([zeros, a2], axis=0).astype(BF16),
        g2=g2.astype(BF16), k_k=k_k.reshape(1, D_MODEL), k_a=k_a.reshape(1, D_MODEL),
        r_k=r_k.reshape(1, D_MODEL))


INV_BASE = 8


def _unit_lower_inverses(l_ms, ti, tj):
    cl = l_ms[0].shape[0]
    base = min(INV_BASE, cl)
    eye = jnp.where(ti == tj, 1.0, 0.0)
    diag_blk = ti // base == tj // base
    l_bs = [jnp.where(diag_blk, l_m, 0.0) for l_m in l_ms]
    t_ms = [eye - l_b for l_b in l_bs]
    for _ in range(base.bit_length() - 2):
        l_bs = [_INV_MM(_dot, l_b, l_b) for l_b in l_bs]
        t_ms = [t_m + _INV_MM(_dot, t_m, l_b) for t_m, l_b in zip(t_ms, l_bs)]
    size = base
    while size < cl:
        lower_left = (ti // (2 * size) == tj // (2 * size)) & (ti // size != tj // size)
        tmp = [_INV_MM(_dot, t_m, jnp.where(lower_left, l_m, 0.0)) for t_m, l_m in zip(t_ms, l_ms)]
        t_ms = [t_m - _INV_MM(_dot, x, t_m) for t_m, x in zip(t_ms, tmp)]
        size *= 2
    return t_ms


def _rwkv_chunk_kernel(kap_ref, r_ref, kt_ref, bt_ref, kh_ref, bh_ref, v_ref, dec_ref, s0_ref, o_ref, so_ref,
                       st_scr, *, n_sq):
    c = pl.program_id(1)
    n_c = pl.num_programs(1)
    cl = r_ref.shape[1] // n_sq
    hd = HEAD_DIM
    pairs = range(n_sq * N_PAIRS)
    chunk_of = lambda ref, u: ref[u % N_PAIRS, (u // N_PAIRS) * cl:(u // N_PAIRS + 1) * cl]

    @pl.when(c == 0)
    def _():
        z = jnp.zeros((hd, hd), F32)
        for u in pairs:
            sq, pr = divmod(u, N_PAIRS)
            top = jnp.concatenate([s0_ref[sq, 2 * pr], z], axis=1)
            bot = jnp.concatenate([z, s0_ref[sq, 2 * pr + 1]], axis=1)
            st_scr[u] = jnp.concatenate([top, bot], axis=0)

    ti = lax.broadcasted_iota(jnp.int32, (cl, cl), 0)
    tj = lax.broadcasted_iota(jnp.int32, (cl, cl), 1)
    tril = ti >= tj
    stril = ti > tj
    head1 = lax.broadcasted_iota(jnp.int32, (cl, PAIR_W), 1) >= hd
    sr = lax.broadcasted_iota(jnp.int32, (PAIR_W, PAIR_W), 0)
    sc = lax.broadcasted_iota(jnp.int32, (PAIR_W, PAIR_W), 1)
    same_head = (sr // hd) == (sc // hd)

    hs = [(pr, h) for pr in pairs for h in range(2)]
    bf = lambda xs: [x.astype(BF16) for x in xs]
    r_t = [chunk_of(r_ref, u) for u in pairs]
    kap_b = [chunk_of(kap_ref, u) for u in pairs]
    kt_b = [chunk_of(kt_ref, u) for u in pairs]
    bt_b = [chunk_of(bt_ref, u) for u in pairs]
    v_b = [chunk_of(v_ref, u) for u in pairs]
    masks = (jnp.logical_not(head1), head1)
    r_h = [jnp.where(masks[h], r_t[pr], 0.0) for pr, h in hs]
    r_hb = bf(r_h)
    kap_hb = [jnp.where(masks[h], kap_b[pr], jnp.zeros_like(kap_b[pr])) for pr, h in hs]
    rows4 = [jnp.concatenate([kap_hb[2 * pr], r_hb[2 * pr], kap_hb[2 * pr + 1], r_hb[2 * pr + 1]], axis=0)
             for pr in pairs]
    vs_b = [_dot_nt(rows4[pr], bt_b[pr]) for pr in pairs]
    vs_k = [_dot_nt(rows4[pr], kt_b[pr]) for pr in pairs]
    blk4 = lambda x, i, h: x[(2 * h + i) * cl:(2 * h + i + 1) * cl]
    l_m = [jnp.where(stril, blk4(vs_b[pr], 0, h), 0.0) for pr, h in hs]
    a_kk = [jnp.where(stril, blk4(vs_k[pr], 0, h), 0.0) for pr, h in hs]
    a_rk = [jnp.where(tril, blk4(vs_k[pr], 1, h), 0.0) for pr, h in hs]
    a_rb_b = bf([jnp.where(tril, blk4(vs_b[pr], 1, h), 0.0) for pr, h in hs])
    a_v = [_dot(jnp.concatenate(bf([a_kk[2 * pr], a_rk[2 * pr], a_kk[2 * pr + 1], a_rk[2 * pr + 1]]), axis=0),
                v_b[pr]) for pr in pairs]
    av = [blk4(a_v[pr], 0, h) for pr, h in hs]
    oi_h = [blk4(a_v[pr], 1, h) for pr, h in hs]
    t_b = bf(_unit_lower_inverses(l_m, ti, tj))
    wu = [_dot(t, jnp.concatenate([k, x], axis=1)) for t, k, x in zip(t_b, kap_hb, bf(av))]
    w_h = [x[:, :PAIR_W] for x in wu]
    u_h = [x[:, PAIR_W:] for x in wu]
    corr = [_dot(a, x.astype(BF16)) for a, x in zip(a_rb_b, wu)]
    rp_h = [r - x[:, :PAIR_W] for r, x in zip(r_h, corr)]
    oi_h = [o - x[:, PAIR_W:] for o, x in zip(oi_h, corr)]
    w_m = bf([w_h[2 * pr] + w_h[2 * pr + 1] for pr in pairs])
    u_m = bf([jnp.where(head1, u_h[2 * pr + 1], u_h[2 * pr]) for pr in pairs])
    rp = [rp_h[2 * pr] + rp_h[2 * pr + 1] for pr in pairs]
    oi = [jnp.where(head1, oi_h[2 * pr + 1], oi_h[2 * pr]) for pr in pairs]
    bh_b = [chunk_of(bh_ref, u) for u in pairs]
    kh_b = [chunk_of(kh_ref, u) for u in pairs]
    wtb = [jnp.where(same_head, _dot_tn(w_m[pr], bh_b[pr]), 0.0) for pr in pairs]
    n_t = [jnp.where(same_head, _dot_tn(v_b[pr], kh_b[pr]) - _dot_tn(u_m[pr], bh_b[pr]), 0.0) for pr in pairs]
    s = [st_scr[pr] for pr in pairs]
    o_new = [_mm3(_dot_nt, rp[pr], s[pr]) + oi[pr] for pr in pairs]
    dec = [dec_ref[u // N_PAIRS, :, (u % N_PAIRS) * PAIR_W:(u % N_PAIRS + 1) * PAIR_W] for u in pairs]
    s_new = [s[pr] * dec[pr] - _mm3(_dot, s[pr], wtb[pr]) + n_t[pr] for pr in pairs]
    for u in pairs:
        sq, pr = divmod(u, N_PAIRS)
        o_ref[pr, sq * cl:(sq + 1) * cl] = o_new[u]
        st_scr[u] = s_new[u]

    @pl.when(c == n_c - 1)
    def _():
        for u in pairs:
            sq, pr = divmod(u, N_PAIRS)
            s = st_scr[u]
            so_ref[sq, 2 * pr] = s[:hd, :hd]
            so_ref[sq, 2 * pr + 1] = s[hd:, hd:]


def _rwkv_chunks(pm_inputs, dec, s0, *, n_seq, chunk, seqs_per_step=1):
    rows = pm_inputs[0].shape[1]
    n_c = rows // n_seq // chunk
    n_sq = seqs_per_step
    assert n_sq == 1 or (n_c == 1 and n_seq % n_sq == 0)
    pm_spec = pl.BlockSpec((N_PAIRS, n_sq * chunk, PAIR_W), lambda s, c: (0, s * n_c + c, 0))
    st_spec = pl.BlockSpec((n_sq, N_HEADS, HEAD_DIM, HEAD_DIM), lambda s, c: (s, 0, 0, 0))
    return pl.pallas_call(
        functools.partial(_rwkv_chunk_kernel, n_sq=n_sq),
        out_shape=(jax.ShapeDtypeStruct((N_PAIRS, rows, PAIR_W), F32),
                   jax.ShapeDtypeStruct((n_seq, N_HEADS, HEAD_DIM, HEAD_DIM), F32)),
        grid=(n_seq // n_sq, n_c),
        in_specs=[pm_spec] * 7 + [pl.BlockSpec((n_sq, 1, D_MODEL), lambda s, c: (s * n_c + c, 0, 0)), st_spec],
        out_specs=(pm_spec, st_spec),
        scratch_shapes=[pltpu.VMEM((n_sq * N_PAIRS, PAIR_W, PAIR_W), F32)],
        compiler_params=_cparams("parallel", "arbitrary"),
        name="rwkv_chunks",
    )(*pm_inputs, dec, s0)


def _mixer_out_kernel(x_ref, gt_ref, npost_ref, o_ref, bonus_ref, g_ref, attn_ref, ga_ref, gb_ref,
                      lnw_ref, lnb_ref, wor_ref, wom_ref, wout_ref, y_ref):
    ones3 = _head_ones3()
    parts = []
    for pr in range(N_PAIRS):
        o = o_ref[pr]
        mu = _head_sums(o, ones3) * (1.0 / HEAD_DIM)
        d = o - mu
        var = _head_sums(d * d, ones3) * (1.0 / HEAD_DIM)
        parts.append(d * lax.rsqrt(var + LNX_EPS))
    on = jnp.concatenate(parts, axis=1) * lnw_ref[...] + lnb_ref[...]
    ya = _dot(((on + bonus_ref[...]) * g_ref[...]).astype(BF16), wor_ref[...])
    yb = _dot(attn_ref[...], wom_ref[...])
    y = jax.nn.sigmoid(ga_ref[...]) * ya + jax.nn.sigmoid(gb_ref[...]) * yb
    y = _dot(y.astype(BF16), wout_ref[...])
    y_ref[...] = x_ref[...] + gt_ref[...] * _rms(y, npost_ref[...])


def _mixer_out(x, mod, which, n_post, o_pm, bonus, g, attn, gates, lnx_w, lnx_b, w_o_rwkv, w_o_moba, w_out,
               *, tm, tiles_per_group):
    rows = x.shape[0]
    row_spec = pl.BlockSpec((tm, D_MODEL), lambda i: (i, 0))
    w_spec = pl.BlockSpec((D_MODEL, D_MODEL), lambda i: (0, 0))
    return pl.pallas_call(
        _mixer_out_kernel,
        out_shape=jax.ShapeDtypeStruct((rows, D_MODEL), F32),
        grid=(rows // tm,),
        in_specs=[
            row_spec,
            _mod_spec(mod, which, tm, tiles_per_group),
            _vec_spec(),
            pl.BlockSpec((N_PAIRS, tm, PAIR_W), lambda i: (0, i, 0)),
            row_spec, row_spec, row_spec,
            pl.BlockSpec((tm, D_MODEL), lambda i: (i, 0)),
            pl.BlockSpec((tm, D_MODEL), lambda i: (i, 1)),
            _vec_spec(), _vec_spec(), w_spec, w_spec, w_spec,
        ],
        out_specs=row_spec,
        compiler_params=_cparams("parallel"),
        name="mixer_out",
    )(x, mod, n_post, o_pm, bonus, g, attn, gates, gates, lnx_w, lnx_b, w_o_rwkv, w_o_moba, w_out)


def _rope_rows_kernel(qkv_ref, cos_ref, sin_ref, q_ref, k_ref):
    cos = cos_ref[...]
    sin = sin_ref[...]
    for pr in range(N_PAIRS):
        sl = slice(pr * PAIR_W, (pr + 1) * PAIR_W)
        q_ref[:, sl] = _rope_pair(qkv_ref[:, sl], cos, sin)
        k_ref[:, sl] = _rope_pair(qkv_ref[:, D_MODEL + pr * PAIR_W:D_MODEL + (pr + 1) * PAIR_W], cos, sin)


def _rope_rows(qkv, cos_t, sin_t):
    rows = qkv.shape[0]
    out = jax.ShapeDtypeStruct((rows, D_MODEL), F32)
    return pl.pallas_call(_rope_rows_kernel, out_shape=(out, out), name="rope_rows")(qkv, cos_t, sin_t)


TOK_SLOTS = 8
Q_TOKENS = 4

PAGES_PER_STEP = 8
PAGES_PER_BLOCK = MOBA_BLOCK // PAGE_SIZE


def _moba_sample_kernel(pt_ref, q_ref, kn_ref, vn_ref, *refs, n_seq, n_pages, n_tok):
    pps = PAGES_PER_STEP
    kp_refs, vp_refs = refs[:pps], refs[pps:2 * pps]
    o_ref, q2_scr, s_scr, pn_scr, acc_scr, l_scr = refs[2 * pps:]
    b = pl.program_id(0)
    n = pl.program_id(1)
    n_steps = pl.num_programs(1)
    rows = N_HEADS * Q_TOKENS
    n_blk = n_pages // PAGES_PER_BLOCK
    blk_per_step = pps // PAGES_PER_BLOCK
    scale = HEAD_DIM ** -0.5
    row_head = lax.broadcasted_iota(jnp.int32, (rows, D_MODEL), 0) % N_HEADS
    lane_head = lax.broadcasted_iota(jnp.int32, (rows, D_MODEL), 1) // HEAD_DIM
    slot_k = b % 2
    slot_v = (b + 1) % 2

    @pl.when((n == 0) & (b < n_seq))
    def _():
        q_rep = jnp.concatenate([jnp.broadcast_to(q_ref[t:t + 1, :], (N_HEADS, D_MODEL)) for t in range(Q_TOKENS)],
                                axis=0)
        q_bd = jnp.where(row_head == lane_head, q_rep, 0.0)
        hi, lo = _hi_lo(q_bd)
        q2_scr[:rows] = hi
        q2_scr[rows:] = lo

    @pl.when(b < n_seq)
    def _():
        kh, kl = _hi_lo(jnp.concatenate([r[...] for r in kp_refs], axis=1))
        top = _dot(q2_scr[...], kh)
        s = top[:rows] + top[rows:] + _dot(q2_scr[:rows], kl)
        for i in range(blk_per_step):
            s_scr[slot_k, n * blk_per_step + i] = s[:, i * MOBA_BLOCK:(i + 1) * MOBA_BLOCK]

    @pl.when((n == n_steps - 1) & (b < n_seq))
    def _():
        cols = [jnp.sum(s_scr[slot_k, i], axis=1, keepdims=True) for i in range(n_blk)]
        gate = jnp.concatenate(cols, axis=1)
        col = lax.broadcasted_iota(jnp.int32, gate.shape, 1)
        cnt = jnp.zeros(gate.shape, jnp.int32)
        for n in range(n_blk):
            gn = gate[:, n:n + 1]
            cnt = cnt + jnp.where((gn > gate) | ((gn == gate) & (n < col)), 1, 0)
        bias = jnp.where(cnt < MOBA_TOPK, 0.0, NEG_INF)
        knh, knl = _hi_lo(kn_ref[...])
        qh = q2_scr[:rows]
        s_new = (_dot_nt(qh, knh) + _dot_nt(q2_scr[rows:], knh) + _dot_nt(qh, knl)) * scale
        t_q = lax.broadcasted_iota(jnp.int32, s_new.shape, 0) // N_HEADS
        t_k = lax.broadcasted_iota(jnp.int32, s_new.shape, 1)
        s_new = jnp.where((t_k <= t_q) & (t_k < n_tok), s_new, NEG_INF)
        logits = [s_scr[slot_k, i] * scale + bias[:, i:i + 1] for i in range(n_blk)]
        mx = functools.reduce(jnp.maximum, logits)
        m = jnp.maximum(jnp.max(mx, axis=1, keepdims=True), jnp.max(s_new, axis=1, keepdims=True))
        p_new = jnp.exp(s_new - m)
        ps = [jnp.exp(x - m) for x in logits]
        for i in range(n_blk):
            s_scr[slot_k, i] = ps[i]
        tot = functools.reduce(jnp.add, ps)
        l_scr[slot_k] = jnp.sum(tot, axis=1, keepdims=True) + jnp.sum(p_new, axis=1, keepdims=True)
        pn_scr[slot_k] = p_new

    @pl.when(b >= 1)
    def _():
        @pl.when(n == 0)
        def _():
            acc_scr[...] = _dot(pn_scr[slot_v], vn_ref[...])

        for i in range(blk_per_step):
            vb = jnp.concatenate([r[...] for r in vp_refs[i * PAGES_PER_BLOCK:(i + 1) * PAGES_PER_BLOCK]], axis=1)
            acc_scr[...] += _dot_nt(s_scr[slot_v, n * blk_per_step + i].astype(BF16), vb.astype(BF16))

        @pl.when(n == n_steps - 1)
        def _():
            acc = jnp.where(row_head == lane_head, acc_scr[...] / l_scr[slot_v], 0.0)
            tok = [jnp.sum(acc[t * N_HEADS:(t + 1) * N_HEADS], axis=0, keepdims=True) for t in range(Q_TOKENS)]
            o_ref[...] = jnp.concatenate(tok + [jnp.zeros((TOK_SLOTS - Q_TOKENS, D_MODEL), F32)], axis=0)


def _moba_sample(page_table, q_rot, k_new, v_new, cache_kt, cache_vt, *, n_tok):
    n_seq, n_pages = page_table.shape
    pps = PAGES_PER_STEP
    assert n_pages % pps == 0 and pps % PAGES_PER_BLOCK == 0 and n_tok <= Q_TOKENS
    rows = N_HEADS * Q_TOKENS
    n_blk = n_pages // PAGES_PER_BLOCK
    cur = lambda b: jnp.minimum(b, n_seq - 1)
    prev = lambda b: jnp.maximum(b - 1, 0)
    cur_spec = pl.BlockSpec((None, TOK_SLOTS, D_MODEL), lambda b, n, pt: (cur(b), 0, 0))
    prev_spec = pl.BlockSpec((None, TOK_SLOTS, D_MODEL), lambda b, n, pt: (prev(b), 0, 0))

    def page_spec(seq_of, i):
        return pl.BlockSpec((None, D_MODEL, PAGE_SIZE),
                            lambda b, n, pt: (pt[seq_of(b) * n_pages + n * pps + i], 0, 0))

    return pl.pallas_call(
        functools.partial(_moba_sample_kernel, n_seq=n_seq, n_pages=n_pages, n_tok=n_tok),
        out_shape=jax.ShapeDtypeStruct((n_seq, TOK_SLOTS, D_MODEL), F32),
        grid_spec=pltpu.PrefetchScalarGridSpec(
            num_scalar_prefetch=1,
            grid=(n_seq + 1, n_pages // pps),
            in_specs=[cur_spec, cur_spec, prev_spec]
            + [page_spec(cur, i) for i in range(pps)] + [page_spec(prev, i) for i in range(pps)],
            out_specs=prev_spec,
            scratch_shapes=[
                pltpu.VMEM((2 * rows, D_MODEL), BF16),
                pltpu.VMEM((2, n_blk, rows, MOBA_BLOCK), F32),
                pltpu.VMEM((2, rows, TOK_SLOTS), F32),
                pltpu.VMEM((rows, D_MODEL), F32),
                pltpu.VMEM((2, rows, 1), F32),
            ],
        ),
        compiler_params=_cparams("arbitrary", "arbitrary"),
        name="moba_sample",
    )(page_table.reshape(-1), q_rot, k_new, v_new, *([cache_kt] * pps), *([cache_vt] * pps))


FFN_TM = 1024
INPROJ_TM = 512
RWKV_TM = 256
PROMPT_CHUNK = 64
SAMPLE_CHUNK = 16
SAMPLE_SEQS_PER_STEP = 4


def _layer(x, mod, lw, *, group_rows, shift_ovr, shift_period, prep_chunk, attend, rwkv_scan):
    tm_ffn, tm_in = min(FFN_TM, group_rows), min(INPROJ_TM, group_rows)
    x1 = _ffn(x, mod, 0, lw["n1_pre"], lw["n1_post"], lw["w1_gu"], lw["w1_down"], tm=tm_ffn,
              tiles_per_group=group_rows // tm_ffn)
    p_rwkv, qkv, gates = _inproj(x1, mod, 3, lw["n2_pre"], lw["w_in"], tm=tm_in,
                                 tiles_per_group=group_rows // tm_in)
    attn, extras = attend(qkv)
    pm, dec, g, bonus = _rwkv_prep(p_rwkv, shift_ovr, lw["rwkv"], tm=RWKV_TM, period=shift_period,
                                   chunk=prep_chunk)
    o_pm, state = rwkv_scan(pm, dec)
    x2 = _mixer_out(x1, mod, 5, lw["n2_post"], o_pm, bonus, g, attn, gates, lw["lnx_w"], lw["lnx_b"],
                    lw["w_o_rwkv"], lw["w_o_moba"], lw["w_out"], tm=RWKV_TM,
                    tiles_per_group=group_rows // RWKV_TM)
    y = _ffn(x2, mod, 6, lw["n3_pre"], lw["n3_post"], lw["w3_gu"], lw["w3_down"], tm=tm_ffn,
             tiles_per_group=group_rows // tm_ffn)
    return y, p_rwkv, qkv, state, extras


def kernel(x_prompt, x_sample, cache_k, cache_v, state_rwkv, state_shift, page_table, c_prompt, c_sample, w_ada, b_ada, n1_pre, n1_post, w1_gu, w1_down, n2_pre, n2_post, w_in, mu_shift, w0, w2_decay, a0, a2, g2, k_k, k_a, r_k, lnx_w, lnx_b, w_o_rwkv, w_o_moba, w_out, n3_pre, n3_post, w3_gu, w3_down):
    assert w_ada.shape[0] == 1, "single layer"
    bp, seq, _ = x_prompt.shape
    bs, ts, _ = x_sample.shape
    n_pool = cache_k.shape[1]
    n_pages = page_table.shape[1]
    rows_p, rows_s = bp * seq, bs * ts
    assert seq % FFN_TM == 0 and seq % MOBA_BLOCK == 0 and rows_s % INPROJ_TM == 0 and ts <= SAMPLE_CHUNK

    lw = dict(
        n1_pre=n1_pre, n1_post=n1_post, n2_pre=n2_pre, n2_post=n2_post, n3_pre=n3_pre, n3_post=n3_post,
        w1_gu=w1_gu[0].astype(BF16), w1_down=w1_down[0].astype(BF16),
        w3_gu=w3_gu[0].astype(BF16), w3_down=w3_down[0].astype(BF16),
        w_in=_split_w_in(w_in[0]),
        rwkv=_rwkv_weights(mu_shift[0], w0[0], w2_decay[0], a0[0], a2[0], g2[0], k_k[0], k_a[0], r_k[0]),
        lnx_w=lnx_w, lnx_b=lnx_b,
        w_o_rwkv=w_o_rwkv[0].astype(BF16), w_o_moba=w_o_moba[0].astype(BF16), w_out=w_out[0].astype(BF16),
    )

    n_c = bp + bs
    c_all = jnp.concatenate([c_prompt, c_sample, jnp.zeros((-n_c % 8, D_MODEL), F32)], axis=0)
    mod = _ada(c_all, w_ada[0], b_ada[0])
    mod_p = mod[:bp].reshape(bp, N_ADA, 1, D_MODEL).transpose(1, 0, 2, 3)
    mod_s = jnp.repeat(mod[bp:n_c].reshape(bs, N_ADA, D_MODEL), ts, axis=0).transpose(1, 0, 2)[:, None]

    cos_p, sin_p = _rope_tables(jnp.arange(seq, dtype=jnp.int32))

    def attend_p(qkv):
        attn, k_rot, v = _moba_prompt(qkv, cos_p, sin_p, bp, seq)
        return attn.reshape(rows_p, D_MODEL), (k_rot, v)

    def scan_p(pm, dec):
        s0 = jnp.zeros((bp, N_HEADS, HEAD_DIM, HEAD_DIM), F32)
        return _rwkv_chunks(pm, dec, s0, n_seq=bp, chunk=PROMPT_CHUNK)

    y_p, p_p, _, state_p, (k_p, v_p) = _layer(
        x_prompt.reshape(rows_p, D_MODEL), mod_p, lw, group_rows=seq,
        shift_ovr=jnp.zeros((bp, 1, SHIFT_W), F32), shift_period=RWKV_TM, prep_chunk=PROMPT_CHUNK,
        attend=attend_p, rwkv_scan=scan_p)

    pos_s = n_pages * PAGE_SIZE + jnp.arange(ts, dtype=jnp.int32)
    cos_s, sin_s = (jnp.tile(t, (bs, 1)) for t in _rope_tables(pos_s))
    pad_tok = lambda a: jnp.pad(a.reshape(bs, ts, D_MODEL), ((0, 0), (0, TOK_SLOTS - ts), (0, 0)))
    cache_kt = cache_k[0].transpose(0, 2, 3, 1).reshape(n_pool, D_MODEL, PAGE_SIZE)
    cache_vt = cache_v[0].transpose(0, 2, 3, 1).reshape(n_pool, D_MODEL, PAGE_SIZE)

    def attend_s(qkv):
        q_rot, k_rot = _rope_rows(qkv, cos_s, sin_s)
        v = qkv[:, 2 * D_MODEL:]
        attn = _moba_sample(page_table, pad_tok(q_rot), pad_tok(k_rot), pad_tok(v), cache_kt, cache_vt, n_tok=ts)
        return attn[:, :ts].reshape(rows_s, D_MODEL).astype(BF16), (k_rot, v)

    def scan_s(pm, dec):
        pad = lambda a: jnp.pad(a.reshape(N_PAIRS, bs, ts, PAIR_W),
                                ((0, 0), (0, 0), (0, SAMPLE_CHUNK - ts), (0, 0))).reshape(N_PAIRS, -1, PAIR_W)
        o_pm, state = _rwkv_chunks([pad(a) for a in pm], dec, state_rwkv[0], n_seq=bs, chunk=SAMPLE_CHUNK,
                                   seqs_per_step=SAMPLE_SEQS_PER_STEP)
        o_pm = o_pm.reshape(N_PAIRS, bs, SAMPLE_CHUNK, PAIR_W)[:, :, :ts].reshape(N_PAIRS, rows_s, PAIR_W)
        return o_pm, state

    y_s, p_s, _, state_s, (k_s, v_s) = _layer(
        x_sample.reshape(rows_s, D_MODEL), mod_s, lw, group_rows=rows_s,
        shift_ovr=jnp.repeat(state_shift[0], ts, axis=0)[None], shift_period=ts, prep_chunk=ts,
        attend=attend_s, rwkv_scan=scan_s)

    heads = lambda a, b, t: a.reshape(1, b, t, N_HEADS, HEAD_DIM)
    return (
        y_p.reshape(bp, seq, D_MODEL),
        y_s.reshape(bs, ts, D_MODEL),
        heads(k_p, bp, seq), heads(v_p, bp, seq),
        state_p[None], p_p.reshape(bp, seq, SHIFT_W)[:, -1][None],
        heads(k_s, bs, ts), heads(v_s, bs, ts),
        state_s[None], p_s.reshape(bs, ts, SHIFT_W)[:, -1][None],
    )
```
